```python
import math
import jax, jax.numpy as jnp
from jax import lax
import numpy as np

D_MODEL = 1024
BATCH = 8
SEQ = 2048
DEPTH = 2

MIX_W = 256
N_BRANCH = 4
CONV_A = 3
SC_HEADS = 4
POOL_WINDOWS = (2, 4, 8, 16)
POOL_GROUPS = len(POOL_WINDOWS)
POOL_GC = MIX_W // POOL_GROUPS
S5_GROUP_CH = 16
S5_GROUPS = MIX_W // S5_GROUP_CH
S5_STATE = 64
S5_EIG_CLIP = -1e-4
LRU_HEADS = 4
LRU_HD = MIX_W // LRU_HEADS
LRU_CONV = 4
LRU_C = 8.0
COLS_A = 3 * MIX_W
COLS_B = MIX_W
COLS_C = MIX_W
COLS_D = 2 * MIX_W
IN_COLS = COLS_A + COLS_B + COLS_C + COLS_D
N_GROUPS = 4
EXP_PER_GROUP = 8
N_EXPERTS = N_GROUPS * EXP_PER_GROUP
TOP_K = 2
D_EXPERT = 512
MOE_BLOCK = 128
EPS = 1e-6

kernel_name = "hybrid_gated_conv_pool_s5_rglru_hmoe"


def rms_norm(x, g):
    x32 = x.astype(jnp.float32)
    y = x32 * lax.rsqrt(jnp.mean(x32 * x32, axis=-1, keepdims=True) + EPS) * g.astype(jnp.float32)
    return y.astype(x.dtype)


def causal_dwconv(u, w):
    k_taps = w.shape[0]
    s = u.shape[1]
    up = jnp.pad(u, ((0, 0), (k_taps - 1, 0), (0, 0)))
    y = up[:, 0:s] * w[0]
    for k in range(1, k_taps):
        y = y + up[:, k:k + s] * w[k]
    return y


def linear_scan(a, b):
    def op(c1, c2):
        a1, b1 = c1
        a2, b2 = c2
        return a1 * a2, a2 * b1 + b2
    _, h = lax.associative_scan(op, (a, b), axis=1)
    return h


def short_conv_mixer(bg, cg, xv, conv_w):
    return bg * causal_dwconv(cg * xv, conv_w)


def pool_mixer(u, pool_w, pool_b, pool_scale):
    bn, s, _ = u.shape
    u32 = u.astype(jnp.float32)
    cs = jnp.cumsum(u32, axis=1)
    t = jnp.arange(s)
    outs = []
    for gi, w in enumerate(POOL_WINDOWS):
        c = cs[..., gi * POOL_GC:(gi + 1) * POOL_GC]
        prev = jnp.pad(c, ((0, 0), (w, 0), (0, 0)))[:, :s]
        cnt = jnp.minimum(t + 1, w).astype(jnp.float32)[None, :, None]
        outs.append((c - prev) / cnt)
    pooled = jnp.concatenate(outs, axis=-1) - u32
    pg = pooled.reshape(bn, s, POOL_GROUPS, POOL_GC)
    y = jnp.einsum('bsgc,gcd->bsgd', pg, pool_w.astype(jnp.float32)) + pool_b.astype(jnp.float32)
    return (y.reshape(bn, s, MIX_W) * pool_scale.astype(jnp.float32)).astype(u.dtype)


def s5_mixer(u, lam_re, lam_im, log_step, b_re, b_im, c_re, c_im, d, glu_w, glu_b):
    bn, s, _ = u.shape
    u32 = u.astype(jnp.float32)
    ug = u32.reshape(bn, s, S5_GROUPS, S5_GROUP_CH)
    lam = lax.complex(jnp.minimum(lam_re.astype(jnp.float32), S5_EIG_CLIP), lam_im.astype(jnp.float32))
    step = jnp.exp(log_step.astype(jnp.float32))[:, None]
    a_bar = jnp.exp(lam * step)
    coef = (a_bar - 1.0) / lam
    b_bar = coef[..., None] * lax.complex(b_re.astype(jnp.float32), b_im.astype(jnp.float32))
    bu = lax.complex(jnp.einsum('bsgh,gph->bsgp', ug, jnp.real(b_bar)),
                     jnp.einsum('bsgh,gph->bsgp', ug, jnp.imag(b_bar)))
    h = linear_scan(jnp.broadcast_to(a_bar, bu.shape), bu)
    y = (jnp.einsum('bsgp,ghp->bsgh', jnp.real(h), c_re.astype(jnp.float32))
         - jnp.einsum('bsgp,ghp->bsgh', jnp.imag(h), c_im.astype(jnp.float32)))
    y = y.reshape(bn, s, MIX_W) + d.astype(jnp.float32) * u32
    y = jax.nn.gelu(y)
    y = y * jax.nn.sigmoid(y @ glu_w.astype(jnp.float32) + glu_b.astype(jnp.float32))
    return y.astype(u.dtype)


def rglru_mixer(xb, gate_in, conv_w, conv_b, wa, ba, wx, bx, lam):
    bn, s, _ = xb.shape
    xc = (causal_dwconv(xb, conv_w) + conv_b).astype(jnp.float32)
    xh = xc.reshape(bn, s, LRU_HEADS, LRU_HD)
    r = jax.nn.sigmoid(jnp.einsum('bshc,hcd->bshd', xh, wa.astype(jnp.float32)) + ba.astype(jnp.float32)).reshape(bn, s, MIX_W)
    i = jax.nn.sigmoid(jnp.einsum('bshc,hcd->bshd', xh, wx.astype(jnp.float32)) + bx.astype(jnp.float32)).reshape(bn, s, MIX_W)
    log_a = -LRU_C * r * jax.nn.softplus(-lam.astype(jnp.float32))
    a = jnp.exp(log_a)
    mult = jnp.sqrt(-jnp.expm1(2.0 * log_a))
    h = linear_scan(a, mult * (i * xc))
    return (h * jax.nn.gelu(gate_in.astype(jnp.float32))).astype(xb.dtype)


def hybrid_mixer(xn, w_in, conv_a_w, pool_w, pool_b, pool_scale,
                 s5_lambda_re, s5_lambda_im, s5_log_step, s5_b_re, s5_b_im, s5_c_re, s5_c_im, s5_d, s5_glu_w, s5_glu_b,
                 lru_conv_w, lru_conv_b, lru_wa, lru_ba, lru_wx, lru_bx, lru_lambda,
                 merge_gate_w, merge_gate_b, branch_w, w_out):
    bn, s, d = xn.shape
    proj = xn @ w_in
    o0 = 0
    a_b = proj[..., o0:o0 + MIX_W]; a_c = proj[..., o0 + MIX_W:o0 + 2 * MIX_W]; a_x = proj[..., o0 + 2 * MIX_W:o0 + 3 * MIX_W]
    o1 = COLS_A
    p_u = proj[..., o1:o1 + COLS_B]
    o2 = o1 + COLS_B
    s_u = proj[..., o2:o2 + COLS_C]
    o3 = o2 + COLS_C
    l_x = proj[..., o3:o3 + MIX_W]; l_g = proj[..., o3 + MIX_W:o3 + 2 * MIX_W]

    ya = short_conv_mixer(a_b, a_c, a_x, conv_a_w)
    yb = pool_mixer(p_u, pool_w, pool_b, pool_scale)
    yc = s5_mixer(s_u, s5_lambda_re, s5_lambda_im, s5_log_step, s5_b_re, s5_b_im, s5_c_re, s5_c_im, s5_d, s5_glu_w, s5_glu_b)
    yd = rglru_mixer(l_x, l_g, lru_conv_w, lru_conv_b, lru_wa, lru_ba, lru_wx, lru_bx, lru_lambda)

    ys = jnp.stack([ya, yb, yc, yd], axis=2)
    branch = jnp.einsum('bskw,kwd->bskd', ys, branch_w)
    gates = jax.nn.sigmoid(xn @ merge_gate_w + merge_gate_b).reshape(bn, s, N_BRANCH, d)
    merged = jnp.sum(gates * branch, axis=2)
    return merged @ w_out


def hier_moe(xn, rg_w, rg_b, re_w, re_b, w1, w3, w2):
    bn, s, d = xn.shape
    x = xn.reshape(-1, d)
    t_tok = x.shape[0]
    x32 = x.astype(jnp.float32)
    g_prob = jax.nn.softmax(x32 @ rg_w.astype(jnp.float32) + rg_b.astype(jnp.float32), axis=-1)
    g_val, g_idx = lax.top_k(g_prob, 1)
    e_logits = (x32 @ re_w.astype(jnp.float32) + re_b.astype(jnp.float32)).reshape(t_tok, N_GROUPS, EXP_PER_GROUP)
    sel = jnp.take_along_axis(e_logits, g_idx[:, :, None], axis=1)[:, 0]
    e_val, e_idx = lax.top_k(sel, TOP_K)
    weights = jax.nn.softmax(e_val, axis=-1) * g_val
    expert_ids = g_idx * EXP_PER_GROUP + e_idx

    tk = t_tok * TOP_K
    flat_e = expert_ids.reshape(-1).astype(jnp.int32)
    flat_w = weights.reshape(-1)
    flat_tok = jnp.repeat(jnp.arange(t_tok, dtype=jnp.int32), TOP_K)
    order = jnp.argsort(flat_e)
    se = flat_e[order]
    counts = jnp.bincount(flat_e, length=N_EXPERTS)
    padded = ((counts + MOE_BLOCK - 1) // MOE_BLOCK) * MOE_BLOCK
    pad_end = jnp.cumsum(padded)
    pad_start = pad_end - padded
    start = jnp.cumsum(counts) - counts
    rank = jnp.arange(tk, dtype=jnp.int32) - start[se]
    dest = pad_start[se] + rank
    n_rows = (-(-tk // MOE_BLOCK)) * MOE_BLOCK + N_EXPERTS * MOE_BLOCK
    n_blocks = n_rows // MOE_BLOCK
    buf_tok = jnp.zeros((n_rows,), jnp.int32).at[dest].set(flat_tok[order])
    buf_w = jnp.zeros((n_rows,), jnp.float32).at[dest].set(flat_w[order])
    block_e = jnp.minimum(jnp.searchsorted(pad_end, jnp.arange(n_blocks) * MOE_BLOCK, side='right'), N_EXPERTS - 1)
    xs = x[buf_tok].reshape(n_blocks, MOE_BLOCK, d)

    def expert_block(args):
        xb, e = args
        h = jax.nn.silu(xb @ w1[e]) * (xb @ w3[e])
        return h @ w2[e]

    ys = lax.map(expert_block, (xs, block_e)).reshape(n_rows, d).astype(jnp.float32) * buf_w[:, None]
    out = jnp.zeros((t_tok, d), jnp.float32).at[buf_tok].add(ys)
    return out.reshape(bn, s, d).astype(xn.dtype)


def setup_inputs(seed: int = 0) -> dict:
    key = jax.random.key(seed)
    ks = iter(jax.random.split(key, 64))
    L, D, W = DEPTH, D_MODEL, MIX_W

    def nrm(shape, scale):
        return jax.random.normal(next(ks), shape, jnp.float32) * scale

    def unif(shape, lo, hi):
        return jax.random.uniform(next(ks), shape, jnp.float32, lo, hi)

    x = nrm((BATCH, SEQ, D), 1.0)
    norm1_g = 1.0 + nrm((L, D), 0.02)
    w_in = nrm((L, D, IN_COLS), D ** -0.5)
    conv_a_w = nrm((L, CONV_A, W), CONV_A ** -0.5)
    pool_w = nrm((L, POOL_GROUPS, POOL_GC, POOL_GC), POOL_GC ** -0.5)
    pool_b = nrm((L, POOL_GROUPS, POOL_GC), 0.01)
    pool_scale = 1.0 + nrm((L, W), 0.02)
    s5_lambda_re = -0.5 + nrm((L, S5_GROUPS, S5_STATE), 0.01)
    s5_lambda_im = math.pi * jnp.broadcast_to(jnp.arange(S5_STATE, dtype=jnp.float32), (L, S5_GROUPS, S5_STATE)) + nrm((L, S5_GROUPS, S5_STATE), 0.01)
    s5_log_step = unif((L, S5_GROUPS), math.log(1e-3), math.log(1e-1))
    s5_b_re = nrm((L, S5_GROUPS, S5_STATE, S5_GROUP_CH), (2.0 * S5_GROUP_CH) ** -0.5)
    s5_b_im = nrm((L, S5_GROUPS, S5_STATE, S5_GROUP_CH), (2.0 * S5_GROUP_CH) ** -0.5)
    s5_c_re = nrm((L, S5_GROUPS, S5_GROUP_CH, S5_STATE), 0.5)
    s5_c_im = nrm((L, S5_GROUPS, S5_GROUP_CH, S5_STATE), 0.5)
    s5_d = nrm((L, W), 0.5)
    s5_glu_w = nrm((L, W, W), W ** -0.5)
    s5_glu_b = nrm((L, W), 0.01)
    lru_conv_w = nrm((L, LRU_CONV, W), LRU_CONV ** -0.5)
    lru_conv_b = nrm((L, W), 0.01)
    lru_wa = nrm((L, LRU_HEADS, LRU_HD, LRU_HD), LRU_HD ** -0.5)
    lru_ba = nrm((L, LRU_HEADS, LRU_HD), 0.01)
    lru_wx = nrm((L, LRU_HEADS, LRU_HD, LRU_HD), LRU_HD ** -0.5)
    lru_bx = nrm((L, LRU_HEADS, LRU_HD), 0.01)
    a0 = unif((L, W), 0.9, 0.999) ** (1.0 / LRU_C)
    lru_lambda = jnp.log(a0) - jnp.log1p(-a0)
    merge_gate_w = nrm((L, D, N_BRANCH * D), D ** -0.5)
    merge_gate_b = nrm((L, N_BRANCH * D), 0.01)
    branch_w = nrm((L, N_BRANCH, W, D), W ** -0.5)
    w_out = nrm((L, D, D), D ** -0.5)
    norm2_g = 1.0 + nrm((L, D), 0.02)
    router_group_w = nrm((L, D, N_GROUPS), D ** -0.5)
    router_group_b = nrm((L, N_GROUPS), 0.01)
    router_expert_w = nrm((L, D, N_EXPERTS), D ** -0.5)
    router_expert_b = nrm((L, N_EXPERTS), 0.01)
    expert_w1 = nrm((L, N_EXPERTS, D, D_EXPERT), D ** -0.5)
    expert_w3 = nrm((L, N_EXPERTS, D, D_EXPERT), D ** -0.5)
    expert_w2 = nrm((L, N_EXPERTS, D_EXPERT, D), D_EXPERT ** -0.5)
    final_norm_g = 1.0 + nrm((D,), 0.02)
    return {
        "x": x, "norm1_g": norm1_g, "w_in": w_in, "conv_a_w": conv_a_w,
        "pool_w": pool_w, "pool_b": pool_b, "pool_scale": pool_scale,
        "s5_lambda_re": s5_lambda_re, "s5_lambda_im": s5_lambda_im, "s5_log_step": s5_log_step,
        "s5_b_re": s5_b_re, "s5_b_im": s5_b_im, "s5_c_re": s5_c_re, "s5_c_im": s5_c_im,
        "s5_d": s5_d, "s5_glu_w": s5_glu_w, "s5_glu_b": s5_glu_b,
        "lru_conv_w": lru_conv_w, "lru_conv_b": lru_conv_b, "lru_wa": lru_wa, "lru_ba": lru_ba,
        "lru_wx": lru_wx, "lru_bx": lru_bx, "lru_lambda": lru_lambda,
        "merge_gate_w": merge_gate_w, "merge_gate_b": merge_gate_b, "branch_w": branch_w, "w_out": w_out,
        "norm2_g": norm2_g, "router_group_w": router_group_w, "router_group_b": router_group_b,
        "router_expert_w": router_expert_w, "router_expert_b": router_expert_b,
        "expert_w1": expert_w1, "expert_w3": expert_w3, "expert_w2": expert_w2,
        "final_norm_g": final_norm_g,
    }


def reference(x, norm1_g, w_in, conv_a_w, pool_w, pool_b, pool_scale,
              s5_lambda_re, s5_lambda_im, s5_log_step, s5_b_re, s5_b_im, s5_c_re, s5_c_im,
              s5_d, s5_glu_w, s5_glu_b, lru_conv_w, lru_conv_b, lru_wa, lru_ba, lru_wx, lru_bx, lru_lambda,
              merge_gate_w, merge_gate_b, branch_w, w_out, norm2_g,
              router_group_w, router_group_b, router_expert_w, router_expert_b,
              expert_w1, expert_w3, expert_w2, final_norm_g):
    h = x
    for l in range(DEPTH):
        xn = rms_norm(h, norm1_g[l])
        h = h + hybrid_mixer(xn, w_in[l], conv_a_w[l], pool_w[l], pool_b[l], pool_scale[l],
                             s5_lambda_re[l], s5_lambda_im[l], s5_log_step[l], s5_b_re[l], s5_b_im[l],
                             s5_c_re[l], s5_c_im[l], s5_d[l], s5_glu_w[l], s5_glu_b[l],
                             lru_conv_w[l], lru_conv_b[l], lru_wa[l], lru_ba[l], lru_wx[l], lru_bx[l], lru_lambda[l],
                             merge_gate_w[l], merge_gate_b[l], branch_w[l], w_out[l])
        xn = rms_norm(h, norm2_g[l])
        h = h + hier_moe(xn, router_group_w[l], router_group_b[l], router_expert_w[l], router_expert_b[l],
                         expert_w1[l], expert_w3[l], expert_w2[l])
    return rms_norm(h, final_norm_g)
```

```python
import functools

import jax
import jax.numpy as jnp
from jax import lax
from jax.experimental import pallas as pl
from jax.experimental.pallas import tpu as pltpu

F32 = jnp.float32
BF16 = jnp.bfloat16

D_MODEL = 1024
MIX_W = 256
N_BRANCH = 4
POOL_WINDOWS = (2, 4, 8, 16)
POOL_GC = 64
S5_GROUPS = 16
S5_GROUP_CH = 16
S5_STATE = 64
S5_LANES = S5_GROUPS * S5_STATE
S5_EIG_CLIP = -1e-4
LRU_HEADS = 4
LRU_C = 8.0
IN_COLS = 1792
N_GROUPS = 4
EXP_PER_GROUP = 8
N_EXPERTS = 32
D_EXPERT = 512
EPS = 1e-6

SUBLANES = 8
HALO = 16
SEQ_TILE = 256
ROUTE_TILE = 512
DISPATCH_TILE = 512
COMBINE_TILE = 256
EXPERT_BLOCK = 128
ROUTER_ROWS = 128
VMEM_LIMIT = 56 * 1024 * 1024

(_R_POOL_B, _R_POOL_SCALE, _R_S5_D, _R_GLU_B, _R_CONV_B, _R_BA, _R_BX, _R_LAM,
 _R_CA0, _R_CA1, _R_CA2, _R_CD0, _R_CD1, _R_CD2, _R_CD3) = range(15)


def _rms(x, g):
    return x * lax.rsqrt(jnp.mean(x * x, axis=-1, keepdims=True) + EPS) * g


def _dot(a, b):
    return jnp.dot(a, b, preferred_element_type=F32)


def _mixer_kernel(h_ref, g1_ref, win_ref, vec_ref, s5v_ref, wpool_ref, bbig_ref, cbig_ref,
                  glu_ref, wa_ref, wx_ref, wg_ref, bg_ref, bw_ref, wout_ref,
                  o_ref,
                  ext_a, ext_b, ext_d, st_ref, la_ref, lb_ref, s5c_ref, lruc_ref,
                  wre_ref, wim_ref, pre_ref, pim_ref, coef_ref):
    ts = h_ref.shape[0]
    s = pl.program_id(1)
    row8 = lax.broadcasted_iota(jnp.int32, (SUBLANES, S5_LANES), 0)

    @pl.when(s == 0)
    def _start_of_sequence():
        zeros_halo = jnp.zeros((HALO, MIX_W), F32)
        ext_a[0:HALO, :] = zeros_halo
        ext_b[0:HALO, :] = zeros_halo
        ext_d[0:HALO, :] = zeros_halo
        s5c_ref[...] = jnp.zeros(s5c_ref.shape, F32)
        lruc_ref[...] = jnp.zeros(lruc_ref.shape, F32)
        lam_re = jnp.minimum(s5v_ref[0:1, :], S5_EIG_CLIP)
        lam_im = s5v_ref[1:2, :]
        dt = jnp.exp(s5v_ref[2:3, :])
        xr = lam_re * dt
        th = lam_im * dt
        ea = jnp.exp(xr)
        a_re = ea * jnp.cos(th)
        a_im = ea * jnp.sin(th)
        den = lam_re * lam_re + lam_im * lam_im
        coef_ref[0:1, :] = ((a_re - 1.0) * lam_re + a_im * lam_im) / den
        coef_ref[1:2, :] = (a_im * lam_re - (a_re - 1.0) * lam_im) / den
        kk = (row8 + 1).astype(F32)
        ek = jnp.exp(kk * xr)
        pre_ref[...] = ek * jnp.cos(kk * th)
        pim_ref[...] = ek * jnp.sin(kk * th)
        for j, sh in enumerate((1, 2, 4)):
            e2 = jnp.exp(float(sh) * xr)
            keep = row8 >= sh
            wre_ref[j * SUBLANES:(j + 1) * SUBLANES, :] = jnp.where(keep, e2 * jnp.cos(float(sh) * th), 0.0)
            wim_ref[j * SUBLANES:(j + 1) * SUBLANES, :] = jnp.where(keep, e2 * jnp.sin(float(sh) * th), 0.0)

    def vrow(r):
        return vec_ref[r:r + 1, :]

    h = h_ref[...]
    xn = _rms(h, g1_ref[...])
    xnb = xn.astype(BF16)
    proj = _dot(xnb, win_ref[...])
    a_b = proj[:, 0:256]
    a_c = proj[:, 256:512]
    a_x = proj[:, 512:768]
    p_u = proj[:, 768:1024]
    s_u = proj[:, 1024:1280]
    l_x = proj[:, 1280:1536]
    l_g = proj[:, 1536:1792]

    ext_a[HALO:HALO + ts, :] = a_c * a_x
    conv = ext_a[HALO - 2:HALO - 2 + ts, :] * vrow(_R_CA0)
    conv = conv + ext_a[HALO - 1:HALO - 1 + ts, :] * vrow(_R_CA1)
    conv = conv + ext_a[HALO:HALO + ts, :] * vrow(_R_CA2)
    ya = a_b * conv
    ext_a[0:HALO, :] = ext_a[ts:ts + HALO, :]

    ext_b[HALO:HALO + ts, :] = p_u
    lane = lax.broadcasted_iota(jnp.int32, (1, MIX_W), 1)
    grp = jnp.right_shift(lane, 6)
    win = p_u
    acc = p_u
    sh = 1
    for gi, w in enumerate(POOL_WINDOWS):
        while sh < w:
            acc = acc + ext_b[HALO - sh:HALO - sh + ts, :]
            sh += 1
        if gi > 0:
            win = jnp.where(grp >= gi, acc, win)
        else:
            win = acc
    wlane = jnp.where(grp == 0, 2.0, jnp.where(grp == 1, 4.0, jnp.where(grp == 2, 8.0, 16.0)))
    tpos = (s * ts + lax.broadcasted_iota(jnp.int32, (ts, MIX_W), 0) + 1).astype(F32)
    cnt = jnp.minimum(tpos, wlane)
    pooled = win / cnt - p_u
    yb = (_dot(pooled.astype(BF16), wpool_ref[...]) + vrow(_R_POOL_B)) * vrow(_R_POOL_SCALE)
    ext_b[0:HALO, :] = ext_b[ts:ts + HALO, :]

    bu = _dot(s_u.astype(BF16), bbig_ref[...])
    bre = bu[:, :S5_LANES]
    bim = bu[:, S5_LANES:]
    c_re = coef_ref[0:1, :]
    c_im = coef_ref[1:2, :]
    st_ref[:, :S5_LANES] = c_re * bre - c_im * bim
    st_ref[:, S5_LANES:] = c_re * bim + c_im * bre

    def s5_body(j, carry):
        cr, ci = carry
        r0 = pl.multiple_of(j * SUBLANES, SUBLANES)
        xr_ = st_ref[pl.ds(r0, SUBLANES), :S5_LANES]
        xi_ = st_ref[pl.ds(r0, SUBLANES), S5_LANES:]
        for jj, shift in enumerate((1, 2, 4)):
            wr = wre_ref[jj * SUBLANES:(jj + 1) * SUBLANES, :]
            wi = wim_ref[jj * SUBLANES:(jj + 1) * SUBLANES, :]
            sr = pltpu.roll(xr_, shift, axis=0)
            si = pltpu.roll(xi_, shift, axis=0)
            xr_, xi_ = xr_ + (wr * sr - wi * si), xi_ + (wr * si + wi * sr)
        pr = pre_ref[...]
        pi = pim_ref[...]
        xr_ = xr_ + (pr * cr - pi * ci)
        xi_ = xi_ + (pr * ci + pi * cr)
        st_ref[pl.ds(r0, SUBLANES), :S5_LANES] = xr_
        st_ref[pl.ds(r0, SUBLANES), S5_LANES:] = xi_
        return xr_[SUBLANES - 1:SUBLANES, :], xi_[SUBLANES - 1:SUBLANES, :]

    cr_f, ci_f = lax.fori_loop(0, ts // SUBLANES, s5_body,
                               (s5c_ref[0:1, :S5_LANES], s5c_ref[0:1, S5_LANES:]))
    s5c_ref[0:1, :S5_LANES] = cr_f
    s5c_ref[0:1, S5_LANES:] = ci_f
    yc = (_dot(st_ref[:, :S5_LANES].astype(BF16), cbig_ref[:S5_LANES, :])
          + _dot(st_ref[:, S5_LANES:].astype(BF16), cbig_ref[S5_LANES:, :]))
    yc = yc + vrow(_R_S5_D) * s_u
    yc = jax.nn.gelu(yc)
    yc = yc * jax.nn.sigmoid(_dot(yc.astype(BF16), glu_ref[...]) + vrow(_R_GLU_B))

    ext_d[HALO:HALO + ts, :] = l_x
    xc = ext_d[HALO - 3:HALO - 3 + ts, :] * vrow(_R_CD0)
    xc = xc + ext_d[HALO - 2:HALO - 2 + ts, :] * vrow(_R_CD1)
    xc = xc + ext_d[HALO - 1:HALO - 1 + ts, :] * vrow(_R_CD2)
    xc = xc + ext_d[HALO:HALO + ts, :] * vrow(_R_CD3)
    xc = xc + vrow(_R_CONV_B)
    ext_d[0:HALO, :] = ext_d[ts:ts + HALO, :]
    xcb = xc.astype(BF16)
    r_gate = jax.nn.sigmoid(_dot(xcb, wa_ref[...]) + vrow(_R_BA))
    i_gate = jax.nn.sigmoid(_dot(xcb, wx_ref[...]) + vrow(_R_BX))
    z = -vrow(_R_LAM)
    softplus = jnp.maximum(z, 0.0) + jnp.log1p(jnp.exp(-jnp.abs(z)))
    log_a = -LRU_C * r_gate * softplus
    a_t = jnp.exp(log_a)
    mult = jnp.sqrt(1.0 - a_t * a_t)
    la_ref[...] = a_t
    lb_ref[...] = mult * (i_gate * xc)
    row8w = lax.broadcasted_iota(jnp.int32, (SUBLANES, MIX_W), 0)

    def lru_body(j, carry):
        r0 = pl.multiple_of(j * SUBLANES, SUBLANES)
        a = la_ref[pl.ds(r0, SUBLANES), :]
        b = lb_ref[pl.ds(r0, SUBLANES), :]
        for shift in (1, 2, 4):
            keep = row8w >= shift
            a_s = pltpu.roll(a, shift, axis=0)
            b_s = pltpu.roll(b, shift, axis=0)
            b = jnp.where(keep, a * b_s + b, b)
            a = jnp.where(keep, a * a_s, a)
        hh = a * carry + b
        lb_ref[pl.ds(r0, SUBLANES), :] = hh
        return hh[SUBLANES - 1:SUBLANES, :]

    lru_f = lax.fori_loop(0, ts // SUBLANES, lru_body, lruc_ref[0:1, :])
    lruc_ref[0:1, :] = lru_f
    yd = lb_ref[...] * jax.nn.gelu(l_g)

    merged = None
    for k, yk in enumerate((ya, yb, yc, yd)):
        gate = jax.nn.sigmoid(_dot(xnb, wg_ref[:, k * D_MODEL:(k + 1) * D_MODEL])
                              + bg_ref[:, k * D_MODEL:(k + 1) * D_MODEL])
        term = gate * _dot(yk.astype(BF16), bw_ref[k])
        merged = term if merged is None else merged + term
    o_ref[...] = h + _dot(merged.astype(BF16), wout_ref[...])


def _const_spec(shape):
    nd = len(shape)
    return pl.BlockSpec(shape, lambda b, s: (0,) * nd)


def _mixer(h, p, batch, seq):
    ts = SEQ_TILE
    ns = seq // ts
    t = batch * seq
    weights = (p["g1"], p["w_in"], p["vec"], p["s5v"], p["wpool"], p["bbig"], p["cbig"], p["glu_w"],
               p["wa"], p["wx"], p["wg"], p["bg"], p["bw"], p["w_out"])
    in_specs = [pl.BlockSpec((ts, D_MODEL), lambda b, s: (b * ns + s, 0))]
    in_specs += [_const_spec(w.shape) for w in weights]
    return pl.pallas_call(
        _mixer_kernel,
        grid=(batch, ns),
        in_specs=in_specs,
        out_specs=pl.BlockSpec((ts, D_MODEL), lambda b, s: (b * ns + s, 0)),
        out_shape=jax.ShapeDtypeStruct((t, D_MODEL), F32),
        scratch_shapes=[
            pltpu.VMEM((HALO + ts, MIX_W), F32),
            pltpu.VMEM((HALO + ts, MIX_W), F32),
            pltpu.VMEM((HALO + ts, MIX_W), F32),
            pltpu.VMEM((ts, 2 * S5_LANES), F32),
            pltpu.VMEM((ts, MIX_W), F32),
            pltpu.VMEM((ts, MIX_W), F32),
            pltpu.VMEM((SUBLANES, 2 * S5_LANES), F32),
            pltpu.VMEM((SUBLANES, MIX_W), F32),
            pltpu.VMEM((3 * SUBLANES, S5_LANES), F32),
            pltpu.VMEM((3 * SUBLANES, S5_LANES), F32),
            pltpu.VMEM((SUBLANES, S5_LANES), F32),
            pltpu.VMEM((SUBLANES, S5_LANES), F32),
            pltpu.VMEM((SUBLANES, S5_LANES), F32),
        ],
        compiler_params=pltpu.CompilerParams(
            dimension_semantics=("arbitrary", "arbitrary"),
            vmem_limit_bytes=VMEM_LIMIT),
        name="mixer",
    )(h, *weights)


def _router_kernel(h_ref, g2_ref, wr_ref, br_ref, xn_ref, ri_ref, wcol_ref, cnt_ref, carry_ref):
    tm = h_ref.shape[0]
    i = pl.program_id(0)

    @pl.when(i == 0)
    def _():
        carry_ref[...] = jnp.zeros(carry_ref.shape, F32)

    xn = _rms(h_ref[...], g2_ref[...])
    xn_ref[...] = xn
    logits = lax.dot_general(wr_ref[...], xn.astype(BF16), (((1,), (1,)), ((), ())),
                             preferred_element_type=F32) + br_ref[...]
    row8 = lax.broadcasted_iota(jnp.int32, (SUBLANES, tm), 0)
    neg_inf = jnp.float32(-jnp.inf)
    gl = jnp.where(row8 < N_GROUPS, logits[0:SUBLANES, :], neg_inf)
    gmax = jnp.max(gl, axis=0, keepdims=True)
    ge = jnp.exp(gl - gmax)
    gp = ge / jnp.sum(ge, axis=0, keepdims=True)
    g_val = jnp.max(gp, axis=0, keepdims=True)
    g_idx = jnp.min(jnp.where(gp == g_val, row8, SUBLANES), axis=0, keepdims=True)
    sel = logits[4 * SUBLANES:5 * SUBLANES, :]
    for g in (2, 1, 0):
        sel = jnp.where(g_idx == g, logits[(g + 1) * SUBLANES:(g + 2) * SUBLANES, :], sel)
    v1 = jnp.max(sel, axis=0, keepdims=True)
    i1 = jnp.min(jnp.where(sel == v1, row8, SUBLANES), axis=0, keepdims=True)
    sel2 = jnp.where(row8 == i1, neg_inf, sel)
    v2 = jnp.max(sel2, axis=0, keepdims=True)
    i2 = jnp.min(jnp.where(sel2 == v2, row8, SUBLANES), axis=0, keepdims=True)
    e2 = jnp.exp(v2 - v1)
    denom = 1.0 + e2
    w1 = (1.0 / denom) * g_val
    w2 = (e2 / denom) * g_val
    eid0 = g_idx * EXP_PER_GROUP + i1
    eid1 = g_idx * EXP_PER_GROUP + i2
    e32 = lax.broadcasted_iota(jnp.int32, (N_EXPERTS, tm), 0)
    oh0 = (e32 == eid0).astype(F32)
    oh1 = (e32 == eid1).astype(F32)
    oh = oh0 + oh1
    before = (lax.broadcasted_iota(jnp.int32, (tm, tm), 0)
              < lax.broadcasted_iota(jnp.int32, (tm, tm), 1)).astype(BF16)
    base = _dot(oh.astype(BF16), before) + carry_ref[:, 0:1]
    rank0 = jnp.sum(oh0 * base, axis=0, keepdims=True)
    rank1 = jnp.sum(oh1 * base, axis=0, keepdims=True)
    new_carry = carry_ref[...] + jnp.sum(oh, axis=1, keepdims=True)
    carry_ref[...] = new_carry
    cnt_ref[...] = new_carry.astype(jnp.int32)
    ri_ref[...] = jnp.zeros(ri_ref.shape, jnp.int32)
    ri_ref[0:1, :] = eid0
    ri_ref[1:2, :] = eid1
    ri_ref[2:3, :] = rank0.astype(jnp.int32)
    ri_ref[3:4, :] = rank1.astype(jnp.int32)
    rows = lax.broadcasted_iota(jnp.int32, (ROUTER_ROWS, tm), 0)
    wrows = jnp.where(rows == 0, w1, jnp.where(rows == 1, w2, 0.0))
    wcol_ref[...] = wrows.T


def _router(h, g2, wr_t, br_col):
    t = h.shape[0]
    tm = ROUTE_TILE
    return pl.pallas_call(
        _router_kernel,
        grid=(t // tm,),
        in_specs=[
            pl.BlockSpec((tm, D_MODEL), lambda i: (i, 0)),
            pl.BlockSpec((1, D_MODEL), lambda i: (0, 0)),
            pl.BlockSpec((ROUTER_ROWS, D_MODEL), lambda i: (0, 0)),
            pl.BlockSpec((ROUTER_ROWS, 1), lambda i: (0, 0)),
        ],
        out_specs=[
            pl.BlockSpec((tm, D_MODEL), lambda i: (i, 0)),
            pl.BlockSpec((SUBLANES, tm), lambda i: (0, i)),
            pl.BlockSpec((tm, ROUTER_ROWS), lambda i: (i, 0)),
            pl.BlockSpec((N_EXPERTS, 128), lambda i: (0, 0)),
        ],
        out_shape=[
            jax.ShapeDtypeStruct((t, D_MODEL), F32),
            jax.ShapeDtypeStruct((SUBLANES, t), jnp.int32),
            jax.ShapeDtypeStruct((t, ROUTER_ROWS), F32),
            jax.ShapeDtypeStruct((N_EXPERTS, 128), jnp.int32),
        ],
        scratch_shapes=[pltpu.VMEM((N_EXPERTS, 128), F32)],
        compiler_params=pltpu.CompilerParams(
            dimension_semantics=("arbitrary",), vmem_limit_bytes=VMEM_LIMIT),
        name="router",
    )(h, g2, wr_t, br_col)


def _dispatch_kernel(dest_ref, x_ref, xs_in_ref, xs_ref, sem):
    del xs_in_ref
    tm = x_ref.shape[0]
    base = pl.program_id(0) * tm

    def row_copy(r, d):
        return pltpu.make_async_copy(x_ref.at[pl.ds(r, 1), :], xs_ref.at[pl.ds(d, 1), :], sem)

    def issue(r, _):
        row_copy(r, dest_ref[2 * (base + r)]).start()
        row_copy(r, dest_ref[2 * (base + r) + 1]).start()
        return 0

    lax.fori_loop(0, tm, issue, 0)

    for _ in range(2):
        pltpu.make_async_copy(x_ref, xs_ref.at[pl.ds(0, tm), :], sem).wait()


def _dispatch(dest_flat, xn, xs_zero):
    t = xn.shape[0]
    tm = DISPATCH_TILE
    return pl.pallas_call(
        _dispatch_kernel,
        grid_spec=pltpu.PrefetchScalarGridSpec(
            num_scalar_prefetch=1,
            grid=(t // tm,),
            in_specs=[
                pl.BlockSpec((tm, D_MODEL), lambda i, d: (i, 0)),
                pl.BlockSpec(memory_space=pl.ANY),
            ],
            out_specs=pl.BlockSpec(memory_space=pl.ANY),
            scratch_shapes=[pltpu.SemaphoreType.DMA(())],
        ),
        out_shape=jax.ShapeDtypeStruct(xs_zero.shape, F32),
        input_output_aliases={2: 0},
        compiler_params=pltpu.CompilerParams(dimension_semantics=("arbitrary",)),
        name="dispatch",
    )(dest_flat, xn, xs_zero)


def _expert_kernel(be_ref, meta_ref, xs_ref, w1_ref, w3_ref, w2_ref, y_ref, w1b, w3b, w2b):
    b = pl.program_id(0)

    @pl.when(b >= meta_ref[0])
    def _():
        y_ref[...] = jnp.zeros(y_ref.shape, F32)

    @pl.when(b < meta_ref[0])
    def _():
        e = be_ref[b]
        prev = be_ref[jnp.maximum(b - 1, 0)]

        @pl.when((b == 0) | (e != prev))
        def _():
            w1b[...] = w1_ref[0].astype(BF16)
            w3b[...] = w3_ref[0].astype(BF16)
            w2b[...] = w2_ref[0].astype(BF16)

        xb = xs_ref[...].astype(BF16)
        hid = jax.nn.silu(_dot(xb, w1b[...])) * _dot(xb, w3b[...])
        y_ref[...] = _dot(hid.astype(BF16), w2b[...])


def _experts(block_e, meta, xs, w1, w3, w2):
    n_rows = xs.shape[0]
    bm = EXPERT_BLOCK

    def row_map(b, be, meta):
        return (jnp.minimum(b, meta[0] - 1), 0)

    def w_map(b, be, meta):
        return (be[jnp.minimum(b, meta[0] - 1)], 0, 0)

    return pl.pallas_call(
        _expert_kernel,
        grid_spec=pltpu.PrefetchScalarGridSpec(
            num_scalar_prefetch=2,
            grid=(n_rows // bm,),
            in_specs=[
                pl.BlockSpec((bm, D_MODEL), row_map),
                pl.BlockSpec((1, D_MODEL, D_EXPERT), w_map),
                pl.BlockSpec((1, D_MODEL, D_EXPERT), w_map),
                pl.BlockSpec((1, D_EXPERT, D_MODEL), w_map),
            ],
            out_specs=pl.BlockSpec((bm, D_MODEL), lambda b, be, meta: (b, 0)),
            scratch_shapes=[
                pltpu.VMEM((D_MODEL, D_EXPERT), BF16),
                pltpu.VMEM((D_MODEL, D_EXPERT), BF16),
                pltpu.VMEM((D_EXPERT, D_MODEL), BF16),
            ],
        ),
        out_shape=jax.ShapeDtypeStruct((n_rows, D_MODEL), F32),
        compiler_params=pltpu.CompilerParams(
            dimension_semantics=("arbitrary",), vmem_limit_bytes=VMEM_LIMIT),
        name="experts",
    )(block_e, meta, xs, w1, w3, w2)


def _combine_kernel(dest_ref, h_ref, wcol_ref, g_ref, y_ref, o_ref, buf, sem, *, final_norm):
    tc = h_ref.shape[0]
    i = pl.program_id(0)
    n = pl.num_programs(0)

    def row_copy(p, slot, k, r):
        return pltpu.make_async_copy(y_ref.at[pl.ds(p, 1), :],
                                     buf.at[2 * slot + k, pl.ds(r, 1), :], sem.at[slot])

    def issue(tile, slot):
        def body(r, _):
            tok = tile * tc + r
            row_copy(dest_ref[2 * tok], slot, 0, r).start()
            row_copy(dest_ref[2 * tok + 1], slot, 1, r).start()
            return 0
        lax.fori_loop(0, tc, body, 0)

    @pl.when(i == 0)
    def _():
        issue(0, 0)

    @pl.when(i + 1 < n)
    def _():
        issue(i + 1, (i + 1) % 2)

    slot = i % 2

    for k in range(2):
        pltpu.make_async_copy(y_ref.at[pl.ds(0, tc), :], buf.at[2 * slot + k], sem.at[slot]).wait()
    w = wcol_ref[...]
    out = h_ref[...] + (w[:, 0:1] * buf[2 * slot] + w[:, 1:2] * buf[2 * slot + 1])
    if final_norm:
        out = _rms(out, g_ref[...])
    o_ref[...] = out


def _combine(dest_flat, h, wcol, g, y, final_norm):
    t = h.shape[0]
    tc = COMBINE_TILE
    return pl.pallas_call(
        functools.partial(_combine_kernel, final_norm=final_norm),
        grid_spec=pltpu.PrefetchScalarGridSpec(
            num_scalar_prefetch=1,
            grid=(t // tc,),
            in_specs=[
                pl.BlockSpec((tc, D_MODEL), lambda i, d: (i, 0)),
                pl.BlockSpec((tc, ROUTER_ROWS), lambda i, d: (i, 0)),
                pl.BlockSpec((1, D_MODEL), lambda i, d: (0, 0)),
                pl.BlockSpec(memory_space=pl.ANY),
            ],
            out_specs=pl.BlockSpec((tc, D_MODEL), lambda i, d: (i, 0)),
            scratch_shapes=[
                pltpu.VMEM((4, tc, D_MODEL), F32),
                pltpu.SemaphoreType.DMA((2,)),
            ],
        ),
        out_shape=jax.ShapeDtypeStruct((t, D_MODEL), F32),
        compiler_params=pltpu.CompilerParams(
            dimension_semantics=("arbitrary",), vmem_limit_bytes=VMEM_LIMIT),
        name="combine",
    )(dest_flat, h, wcol, g, y)


def _block_diag(w):
    g, i, o = w.shape
    eye = jnp.eye(g, dtype=w.dtype)
    return jnp.einsum("gio,gk->giko", w, eye).reshape(g * i, g * o)


def _layer_params(l, a):
    row = lambda v: v.reshape(1, -1)
    vec_rows = [a["pool_b"][l].reshape(-1), a["pool_scale"][l], a["s5_d"][l], a["s5_glu_b"][l],
                a["lru_conv_b"][l], a["lru_ba"][l].reshape(-1), a["lru_bx"][l].reshape(-1),
                a["lru_lambda"][l],
                a["conv_a_w"][l][0], a["conv_a_w"][l][1], a["conv_a_w"][l][2],
                a["lru_conv_w"][l][0], a["lru_conv_w"][l][1], a["lru_conv_w"][l][2],
                a["lru_conv_w"][l][3], jnp.zeros((MIX_W,), F32)]
    s5v = jnp.stack([a["s5_lambda_re"][l].reshape(-1), a["s5_lambda_im"][l].reshape(-1),
                     jnp.repeat(a["s5_log_step"][l], S5_STATE)]
                    + [jnp.zeros((S5_LANES,), F32)] * 5)
    b_re = _block_diag(jnp.swapaxes(a["s5_b_re"][l], 1, 2))
    b_im = _block_diag(jnp.swapaxes(a["s5_b_im"][l], 1, 2))
    c_re = _block_diag(jnp.swapaxes(a["s5_c_re"][l], 1, 2))
    c_im = _block_diag(jnp.swapaxes(a["s5_c_im"][l], 1, 2))
    wr_t = jnp.zeros((ROUTER_ROWS, D_MODEL), F32)
    wr_t = wr_t.at[0:N_GROUPS].set(a["router_group_w"][l].T)
    wr_t = wr_t.at[SUBLANES:SUBLANES + N_EXPERTS].set(a["router_expert_w"][l].T)
    br = jnp.zeros((ROUTER_ROWS,), F32)
    br = br.at[0:N_GROUPS].set(a["router_group_b"][l])
    br = br.at[SUBLANES:SUBLANES + N_EXPERTS].set(a["router_expert_b"][l])
    return {
        "g1": row(a["norm1_g"][l]),
        "w_in": a["w_in"][l].astype(BF16),
        "vec": jnp.stack(vec_rows),
        "s5v": s5v,
        "wpool": _block_diag(a["pool_w"][l]).astype(BF16),
        "bbig": jnp.concatenate([b_re, b_im], axis=1).astype(BF16),
        "cbig": jnp.concatenate([c_re, -c_im], axis=0).astype(BF16),
        "glu_w": a["s5_glu_w"][l].astype(BF16),
        "wa": _block_diag(a["lru_wa"][l]).astype(BF16),
        "wx": _block_diag(a["lru_wx"][l]).astype(BF16),
        "wg": a["merge_gate_w"][l].astype(BF16),
        "bg": row(a["merge_gate_b"][l]),
        "bw": a["branch_w"][l].astype(BF16),
        "w_out": a["w_out"][l].astype(BF16),
        "g2": row(a["norm2_g"][l]),
        "wr_t": wr_t.astype(BF16),
        "br": br.reshape(ROUTER_ROWS, 1),
    }


def _moe(h, p, w1, w3, w2, g_final, final_norm):
    t = h.shape[0]
    bm = EXPERT_BLOCK
    xn, route_i, wcol, counts = _router(h, p["g2"], p["wr_t"], p["br"])
    cnt = counts[:, 0]
    padded = ((cnt + bm - 1) // bm) * bm
    pad_end = jnp.cumsum(padded)
    pad_start = pad_end - padded
    dest = pad_start[route_i[0:2]] + route_i[2:4]
    dest_flat = dest.T.reshape(-1).astype(jnp.int32)
    n_rows = (-(-(2 * t) // bm)) * bm + N_EXPERTS * bm
    n_blocks = n_rows // bm
    block_e = jnp.minimum(
        jnp.searchsorted(pad_end, jnp.arange(n_blocks, dtype=jnp.int32) * bm, side="right"),
        N_EXPERTS - 1).astype(jnp.int32)
    meta = (pad_end[-1:] // bm).astype(jnp.int32)
    xs = _dispatch(dest_flat, xn, jnp.zeros((n_rows, D_MODEL), F32))
    y = _experts(block_e, meta, xs, w1, w3, w2)
    return _combine(dest_flat, h, wcol, g_final, y, final_norm)


def kernel(x, norm1_g, w_in, conv_a_w, pool_w, pool_b, pool_scale, s5_lambda_re, s5_lambda_im, s5_log_step, s5_b_re, s5_b_im, s5_c_re, s5_c_im, s5_d, s5_glu_w, s5_glu_b, lru_conv_w, lru_conv_b, lru_wa, lru_ba, lru_wx, lru_bx, lru_lambda, merge_gate_w, merge_gate_b, branch_w, w_out, norm2_g, router_group_w, router_group_b, router_expert_w, router_expert_b, expert_w1, expert_w3, expert_w2, final_norm_g):
    a = dict(norm1_g=norm1_g, w_in=w_in, conv_a_w=conv_a_w, pool_w=pool_w, pool_b=pool_b,
             pool_scale=pool_scale, s5_lambda_re=s5_lambda_re, s5_lambda_im=s5_lambda_im,
             s5_log_step=s5_log_step, s5_b_re=s5_b_re, s5_b_im=s5_b_im, s5_c_re=s5_c_re,
             s5_c_im=s5_c_im, s5_d=s5_d, s5_glu_w=s5_glu_w, s5_glu_b=s5_glu_b,
             lru_conv_w=lru_conv_w, lru_conv_b=lru_conv_b, lru_wa=lru_wa, lru_ba=lru_ba,
             lru_wx=lru_wx, lru_bx=lru_bx, lru_lambda=lru_lambda, merge_gate_w=merge_gate_w,
             merge_gate_b=merge_gate_b, branch_w=branch_w, w_out=w_out, norm2_g=norm2_g,
             router_group_w=router_group_w, router_group_b=router_group_b,
             router_expert_w=router_expert_w, router_expert_b=router_expert_b)
    batch, seq, d = x.shape
    depth = norm1_g.shape[0]
    h = x.reshape(batch * seq, d)
    g_final = final_norm_g.reshape(1, d)
    for l in range(depth):
        p = _layer_params(l, a)
        h = _mixer(h, p, batch, seq)
        h = _moe(h, p, expert_w1[l], expert_w3[l], expert_w2[l], g_final, l == depth - 1)
    return h.reshape(batch, seq, d)
```

```python
import functools

import jax
import jax.numpy as jnp
from jax import lax
from jax.experimental import pallas as pl
from jax.experimental.pallas import tpu as pltpu

F32 = jnp.float32
BF16 = jnp.bfloat16

D_MODEL = 1024
MIX_W = 256
N_BRANCH = 4
POOL_WINDOWS = (2, 4, 8, 16)
POOL_GC = 64
S5_GROUPS = 16
S5_GROUP_CH = 16
S5_STATE = 64
S5_LANES = S5_GROUPS * S5_STATE
S5_EIG_CLIP = -1e-4
LRU_HEADS = 4
LRU_C = 8.0
IN_COLS = 1792
N_GROUPS = 4
EXP_PER_GROUP = 8
N_EXPERTS = 32
D_EXPERT = 512
EPS = 1e-6

SUBLANES = 8
HALO = 16
SEQ_TILE = 256
ROUTE_TILE = 512
DISPATCH_TILE = 512
COMBINE_TILE = 256
EXPERT_BLOCK = 128
ROUTER_ROWS = 128
DEST_TILE = 2048
DMA_UNROLL = 8
PAD_PIECES = tuple(EXPERT_BLOCK >> (k + 1) for k in range(EXPERT_BLOCK.bit_length() - 1))
VMEM_LIMIT = 56 * 1024 * 1024

(_R_POOL_B, _R_POOL_SCALE, _R_S5_D, _R_GLU_B, _R_CONV_B, _R_BA, _R_BX, _R_LAM,
 _R_CA0, _R_CA1, _R_CA2, _R_CD0, _R_CD1, _R_CD2, _R_CD3) = range(15)


def _rms(x, g):
    return x * lax.rsqrt(jnp.mean(x * x, axis=-1, keepdims=True) + EPS) * g


def _dot(a, b):
    return jnp.dot(a, b, preferred_element_type=F32)


def _mixer_kernel(h_ref, g1_ref, win_ref, vec_ref, s5v_ref, wpool_ref, bbig_ref, cbig_ref,
                  glu_ref, wa_ref, wx_ref, wg_ref, bg_ref, bw_ref, wout_ref,
                  o_ref,
                  ext_a, ext_b, ext_d, st_ref, la_ref, lb_ref, s5c_ref, lruc_ref,
                  wre_ref, wim_ref, pre_ref, pim_ref, coef_ref):
    ts = h_ref.shape[0]
    s = pl.program_id(1)
    row8 = lax.broadcasted_iota(jnp.int32, (SUBLANES, S5_LANES), 0)

    @pl.when(s == 0)
    def _start_of_sequence():
        zeros_halo = jnp.zeros((HALO, MIX_W), F32)
        ext_a[0:HALO, :] = zeros_halo
        ext_b[0:HALO, :] = zeros_halo
        ext_d[0:HALO, :] = zeros_halo
        s5c_ref[...] = jnp.zeros(s5c_ref.shape, F32)
        lruc_ref[...] = jnp.zeros(lruc_ref.shape, F32)
        lam_re = jnp.minimum(s5v_ref[0:1, :], S5_EIG_CLIP)
        lam_im = s5v_ref[1:2, :]
        dt = jnp.exp(s5v_ref[2:3, :])
        xr = lam_re * dt
        th = lam_im * dt
        ea = jnp.exp(xr)
        a_re = ea * jnp.cos(th)
        a_im = ea * jnp.sin(th)
        den = lam_re * lam_re + lam_im * lam_im
        coef_ref[0:1, :] = ((a_re - 1.0) * lam_re + a_im * lam_im) / den
        coef_ref[1:2, :] = (a_im * lam_re - (a_re - 1.0) * lam_im) / den
        kk = (row8 + 1).astype(F32)
        ek = jnp.exp(kk * xr)
        pre_ref[...] = ek * jnp.cos(kk * th)
        pim_ref[...] = ek * jnp.sin(kk * th)
        for j, sh in enumerate((1, 2, 4)):
            e2 = jnp.exp(float(sh) * xr)
            keep = row8 >= sh
            wre_ref[j * SUBLANES:(j + 1) * SUBLANES, :] = jnp.where(keep, e2 * jnp.cos(float(sh) * th), 0.0)
            wim_ref[j * SUBLANES:(j + 1) * SUBLANES, :] = jnp.where(keep, e2 * jnp.sin(float(sh) * th), 0.0)

    def vrow(r):
        return vec_ref[r:r + 1, :]

    h = h_ref[...]
    xn = _rms(h, g1_ref[...])
    xnb = xn.astype(BF16)
    proj = _dot(xnb, win_ref[...])
    a_b = proj[:, 0:256]
    a_c = proj[:, 256:512]
    a_x = proj[:, 512:768]
    p_u = proj[:, 768:1024]
    s_u = proj[:, 1024:1280]
    l_x = proj[:, 1280:1536]
    l_g = proj[:, 1536:1792]

    ext_a[HALO:HALO + ts, :] = a_c * a_x
    conv = ext_a[HALO - 2:HALO - 2 + ts, :] * vrow(_R_CA0)
    conv = conv + ext_a[HALO - 1:HALO - 1 + ts, :] * vrow(_R_CA1)
    conv = conv + ext_a[HALO:HALO + ts, :] * vrow(_R_CA2)
    ya = a_b * conv
    ext_a[0:HALO, :] = ext_a[ts:ts + HALO, :]

    ext_b[HALO:HALO + ts, :] = p_u
    lane = lax.broadcasted_iota(jnp.int32, (1, MIX_W), 1)
    grp = jnp.right_shift(lane, 6)
    win = p_u
    acc = p_u
    sh = 1
    for gi, w in enumerate(POOL_WINDOWS):
        while sh < w:
            acc = acc + ext_b[HALO - sh:HALO - sh + ts, :]
            sh += 1
        if gi > 0:
            win = jnp.where(grp >= gi, acc, win)
        else:
            win = acc
    wlane = jnp.where(grp == 0, 2.0, jnp.where(grp == 1, 4.0, jnp.where(grp == 2, 8.0, 16.0)))
    tpos = (s * ts + lax.broadcasted_iota(jnp.int32, (ts, MIX_W), 0) + 1).astype(F32)
    cnt = jnp.minimum(tpos, wlane)
    pooled = win / cnt - p_u
    yb = (_dot(pooled.astype(BF16), wpool_ref[...]) + vrow(_R_POOL_B)) * vrow(_R_POOL_SCALE)
    ext_b[0:HALO, :] = ext_b[ts:ts + HALO, :]

    bu = _dot(s_u.astype(BF16), bbig_ref[...])
    bre = bu[:, :S5_LANES]
    bim = bu[:, S5_LANES:]
    c_re = coef_ref[0:1, :]
    c_im = coef_ref[1:2, :]
    st_ref[:, :S5_LANES] = c_re * bre - c_im * bim
    st_ref[:, S5_LANES:] = c_re * bim + c_im * bre

    def s5_body(j, carry):
        cr, ci = carry
        r0 = pl.multiple_of(j * SUBLANES, SUBLANES)
        xr_ = st_ref[pl.ds(r0, SUBLANES), :S5_LANES]
        xi_ = st_ref[pl.ds(r0, SUBLANES), S5_LANES:]
        for jj, shift in enumerate((1, 2, 4)):
            wr = wre_ref[jj * SUBLANES:(jj + 1) * SUBLANES, :]
            wi = wim_ref[jj * SUBLANES:(jj + 1) * SUBLANES, :]
            sr = pltpu.roll(xr_, shift, axis=0)
            si = pltpu.roll(xi_, shift, axis=0)
            xr_, xi_ = xr_ + (wr * sr - wi * si), xi_ + (wr * si + wi * sr)
        pr = pre_ref[...]
        pi = pim_ref[...]
        xr_ = xr_ + (pr * cr - pi * ci)
        xi_ = xi_ + (pr * ci + pi * cr)
        st_ref[pl.ds(r0, SUBLANES), :S5_LANES] = xr_
        st_ref[pl.ds(r0, SUBLANES), S5_LANES:] = xi_
        return xr_[SUBLANES - 1:SUBLANES, :], xi_[SUBLANES - 1:SUBLANES, :]

    cr_f, ci_f = lax.fori_loop(0, ts // SUBLANES, s5_body,
                               (s5c_ref[0:1, :S5_LANES], s5c_ref[0:1, S5_LANES:]))
    s5c_ref[0:1, :S5_LANES] = cr_f
    s5c_ref[0:1, S5_LANES:] = ci_f
    yc = (_dot(st_ref[:, :S5_LANES].astype(BF16), cbig_ref[:S5_LANES, :])
          + _dot(st_ref[:, S5_LANES:].astype(BF16), cbig_ref[S5_LANES:, :]))
    yc = yc + vrow(_R_S5_D) * s_u
    yc = jax.nn.gelu(yc)
    yc = yc * jax.nn.sigmoid(_dot(yc.astype(BF16), glu_ref[...]) + vrow(_R_GLU_B))

    ext_d[HALO:HALO + ts, :] = l_x
    xc = ext_d[HALO - 3:HALO - 3 + ts, :] * vrow(_R_CD0)
    xc = xc + ext_d[HALO - 2:HALO - 2 + ts, :] * vrow(_R_CD1)
    xc = xc + ext_d[HALO - 1:HALO - 1 + ts, :] * vrow(_R_CD2)
    xc = xc + ext_d[HALO:HALO + ts, :] * vrow(_R_CD3)
    xc = xc + vrow(_R_CONV_B)
    ext_d[0:HALO, :] = ext_d[ts:ts + HALO, :]
    xcb = xc.astype(BF16)
    r_gate = jax.nn.sigmoid(_dot(xcb, wa_ref[...]) + vrow(_R_BA))
    i_gate = jax.nn.sigmoid(_dot(xcb, wx_ref[...]) + vrow(_R_BX))
    z = -vrow(_R_LAM)
    softplus = jnp.maximum(z, 0.0) + jnp.log1p(jnp.exp(-jnp.abs(z)))
    log_a = -LRU_C * r_gate * softplus
    a_t = jnp.exp(log_a)
    mult = jnp.sqrt(1.0 - a_t * a_t)
    la_ref[...] = a_t
    lb_ref[...] = mult * (i_gate * xc)
    row8w = lax.broadcasted_iota(jnp.int32, (SUBLANES, MIX_W), 0)

    def lru_body(j, carry):
        r0 = pl.multiple_of(j * SUBLANES, SUBLANES)
        a = la_ref[pl.ds(r0, SUBLANES), :]
        b = lb_ref[pl.ds(r0, SUBLANES), :]
        for shift in (1, 2, 4):
            keep = row8w >= shift
            a_s = pltpu.roll(a, shift, axis=0)
            b_s = pltpu.roll(b, shift, axis=0)
            b = jnp.where(keep, a * b_s + b, b)
            a = jnp.where(keep, a * a_s, a)
        hh = a * carry + b
        lb_ref[pl.ds(r0, SUBLANES), :] = hh
        return hh[SUBLANES - 1:SUBLANES, :]

    lru_f = lax.fori_loop(0, ts // SUBLANES, lru_body, lruc_ref[0:1, :])
    lruc_ref[0:1, :] = lru_f
    yd = lb_ref[...] * jax.nn.gelu(l_g)

    merged = None
    for k, yk in enumerate((ya, yb, yc, yd)):
        gate = jax.nn.sigmoid(_dot(xnb, wg_ref[:, k * D_MODEL:(k + 1) * D_MODEL])
                              + bg_ref[:, k * D_MODEL:(k + 1) * D_MODEL])
        term = gate * _dot(yk.astype(BF16), bw_ref[k])
        merged = term if merged is None else merged + term
    o_ref[...] = h + _dot(merged.astype(BF16), wout_ref[...])


def _const_spec(shape):
    nd = len(shape)
    return pl.BlockSpec(shape, lambda b, s: (0,) * nd)


def _mixer(h, p, batch, seq):
    ts = SEQ_TILE
    ns = seq // ts
    t = batch * seq
    weights = (p["g1"], p["w_in"], p["vec"], p["s5v"], p["wpool"], p["bbig"], p["cbig"], p["glu_w"],
               p["wa"], p["wx"], p["wg"], p["bg"], p["bw"], p["w_out"])
    in_specs = [pl.BlockSpec((ts, D_MODEL), lambda b, s: (b * ns + s, 0))]
    in_specs += [_const_spec(w.shape) for w in weights]
    return pl.pallas_call(
        _mixer_kernel,
        grid=(batch, ns),
        in_specs=in_specs,
        out_specs=pl.BlockSpec((ts, D_MODEL), lambda b, s: (b * ns + s, 0)),
        out_shape=jax.ShapeDtypeStruct((t, D_MODEL), F32),
        scratch_shapes=[
            pltpu.VMEM((HALO + ts, MIX_W), F32),
            pltpu.VMEM((HALO + ts, MIX_W), F32),
            pltpu.VMEM((HALO + ts, MIX_W), F32),
            pltpu.VMEM((ts, 2 * S5_LANES), F32),
            pltpu.VMEM((ts, MIX_W), F32),
            pltpu.VMEM((ts, MIX_W), F32),
            pltpu.VMEM((SUBLANES, 2 * S5_LANES), F32),
            pltpu.VMEM((SUBLANES, MIX_W), F32),
            pltpu.VMEM((3 * SUBLANES, S5_LANES), F32),
            pltpu.VMEM((3 * SUBLANES, S5_LANES), F32),
            pltpu.VMEM((SUBLANES, S5_LANES), F32),
            pltpu.VMEM((SUBLANES, S5_LANES), F32),
            pltpu.VMEM((SUBLANES, S5_LANES), F32),
        ],
        compiler_params=pltpu.CompilerParams(
            dimension_semantics=("arbitrary", "arbitrary"),
            vmem_limit_bytes=VMEM_LIMIT),
        name="mixer",
    )(h, *weights)


def _router_kernel(h_ref, g2_ref, wr_ref, br_ref, xn_ref, ri_ref, wcol_ref, cnt_ref, carry_ref):
    tm = h_ref.shape[0]
    i = pl.program_id(0)

    @pl.when(i == 0)
    def _():
        carry_ref[...] = jnp.zeros(carry_ref.shape, F32)

    xn = _rms(h_ref[...], g2_ref[...])
    xn_ref[...] = xn
    logits = lax.dot_general(wr_ref[...], xn.astype(BF16), (((1,), (1,)), ((), ())),
                             preferred_element_type=F32) + br_ref[...]
    row8 = lax.broadcasted_iota(jnp.int32, (SUBLANES, tm), 0)
    neg_inf = jnp.float32(-jnp.inf)
    gl = jnp.where(row8 < N_GROUPS, logits[0:SUBLANES, :], neg_inf)
    gmax = jnp.max(gl, axis=0, keepdims=True)
    ge = jnp.exp(gl - gmax)
    gp = ge / jnp.sum(ge, axis=0, keepdims=True)
    g_val = jnp.max(gp, axis=0, keepdims=True)
    g_idx = jnp.min(jnp.where(gp == g_val, row8, SUBLANES), axis=0, keepdims=True)
    sel = logits[4 * SUBLANES:5 * SUBLANES, :]
    for g in (2, 1, 0):
        sel = jnp.where(g_idx == g, logits[(g + 1) * SUBLANES:(g + 2) * SUBLANES, :], sel)
    v1 = jnp.max(sel, axis=0, keepdims=True)
    i1 = jnp.min(jnp.where(sel == v1, row8, SUBLANES), axis=0, keepdims=True)
    sel2 = jnp.where(row8 == i1, neg_inf, sel)
    v2 = jnp.max(sel2, axis=0, keepdims=True)
    i2 = jnp.min(jnp.where(sel2 == v2, row8, SUBLANES), axis=0, keepdims=True)
    e2 = jnp.exp(v2 - v1)
    denom = 1.0 + e2
    w1 = (1.0 / denom) * g_val
    w2 = (e2 / denom) * g_val
    eid0 = g_idx * EXP_PER_GROUP + i1
    eid1 = g_idx * EXP_PER_GROUP + i2
    e32 = lax.broadcasted_iota(jnp.int32, (N_EXPERTS, tm), 0)
    oh0 = (e32 == eid0).astype(F32)
    oh1 = (e32 == eid1).astype(F32)
    oh = oh0 + oh1
    before = (lax.broadcasted_iota(jnp.int32, (tm, tm), 0)
              < lax.broadcasted_iota(jnp.int32, (tm, tm), 1)).astype(BF16)
    base = _dot(oh.astype(BF16), before) + carry_ref[:, 0:1]
    rank0 = jnp.sum(oh0 * base, axis=0, keepdims=True)
    rank1 = jnp.sum(oh1 * base, axis=0, keepdims=True)
    new_carry = carry_ref[...] + jnp.sum(oh, axis=1, keepdims=True)
    carry_ref[...] = new_carry
    cnt_ref[...] = new_carry.astype(jnp.int32)
    ri_ref[...] = jnp.zeros(ri_ref.shape, jnp.int32)
    ri_ref[0:1, :] = eid0
    ri_ref[1:2, :] = eid1
    ri_ref[2:3, :] = rank0.astype(jnp.int32)
    ri_ref[3:4, :] = rank1.astype(jnp.int32)
    rows = lax.broadcasted_iota(jnp.int32, (ROUTER_ROWS, tm), 0)
    wrows = jnp.where(rows == 0, w1, jnp.where(rows == 1, w2, 0.0))
    wcol_ref[...] = wrows.T


def _router(h, g2, wr_t, br_col):
    t = h.shape[0]
    tm = ROUTE_TILE
    return pl.pallas_call(
        _router_kernel,
        grid=(t // tm,),
        in_specs=[
            pl.BlockSpec((tm, D_MODEL), lambda i: (i, 0)),
            pl.BlockSpec((1, D_MODEL), lambda i: (0, 0)),
            pl.BlockSpec((ROUTER_ROWS, D_MODEL), lambda i: (0, 0)),
            pl.BlockSpec((ROUTER_ROWS, 1), lambda i: (0, 0)),
        ],
        out_specs=[
            pl.BlockSpec((tm, D_MODEL), lambda i: (i, 0)),
            pl.BlockSpec((SUBLANES, tm), lambda i: (0, i)),
            pl.BlockSpec((tm, ROUTER_ROWS), lambda i: (i, 0)),
            pl.BlockSpec((N_EXPERTS, 128), lambda i: (0, 0)),
        ],
        out_shape=[
            jax.ShapeDtypeStruct((t, D_MODEL), F32),
            jax.ShapeDtypeStruct((SUBLANES, t), jnp.int32),
            jax.ShapeDtypeStruct((t, ROUTER_ROWS), F32),
            jax.ShapeDtypeStruct((N_EXPERTS, 128), jnp.int32),
        ],
        scratch_shapes=[pltpu.VMEM((N_EXPERTS, 128), F32)],
        compiler_params=pltpu.CompilerParams(
            dimension_semantics=("arbitrary",), vmem_limit_bytes=VMEM_LIMIT),
        name="router",
    )(h, g2, wr_t, br_col)


def _dest_kernel(ri_ref, ps_ref, o_ref):
    tm = ri_ref.shape[1]
    e32 = lax.broadcasted_iota(jnp.int32, (N_EXPERTS, tm), 0)
    o_ref[...] = jnp.zeros(o_ref.shape, jnp.int32)
    for k in range(2):
        start = jnp.sum(jnp.where(e32 == ri_ref[k:k + 1, :], ps_ref[...], 0.0), axis=0, keepdims=True)
        o_ref[k:k + 1, :] = start.astype(jnp.int32) + ri_ref[2 + k:3 + k, :]


def _dest(route_i, pad_start_col):
    t = route_i.shape[1]
    tm = DEST_TILE
    return pl.pallas_call(
        _dest_kernel,
        grid=(t // tm,),
        in_specs=[
            pl.BlockSpec((SUBLANES, tm), lambda i: (0, i)),
            pl.BlockSpec((N_EXPERTS, 1), lambda i: (0, 0)),
        ],
        out_specs=pl.BlockSpec((SUBLANES, tm), lambda i: (0, i)),
        out_shape=jax.ShapeDtypeStruct((SUBLANES, t), jnp.int32),
        compiler_params=pltpu.CompilerParams(dimension_semantics=("arbitrary",)),
        name="dest",
    )(route_i, pad_start_col)


def _dispatch_kernel(dest_ref, fs_ref, fl_ref, meta_ref, x_ref, xs_ref, zbuf, sem, zsem):
    tm = x_ref.shape[0]
    bm = zbuf.shape[0]
    n_blocks = xs_ref.shape[0] // bm
    i = pl.program_id(0)
    n_tok = pl.num_programs(0) * tm
    base = i * tm

    def pad_copy(off, p):
        return pltpu.make_async_copy(zbuf.at[pl.ds(0, p), :], xs_ref.at[pl.ds(off, p), :], zsem)

    def for_each_fill(act):
        def segment(e, _):
            off = fs_ref[e]
            n = fl_ref[e]
            head = n & (SUBLANES - 1)
            for j in range(SUBLANES - 1):
                @pl.when(j < head)
                def _(off=off, j=j):
                    act(pad_copy(off + j, 1))

            off = pl.multiple_of(off + head, SUBLANES)
            for p in PAD_PIECES:
                if p < SUBLANES:
                    continue
                piece = n & p

                @pl.when(piece != 0)
                def _(off=off, p=p):
                    act(pad_copy(off, p))

                off = pl.multiple_of(off + piece, SUBLANES)
            return 0

        lax.fori_loop(0, N_EXPERTS, segment, 0)

        def tail(b, _):
            act(pad_copy(pl.multiple_of(b * bm, bm), bm))
            return 0

        lax.fori_loop(meta_ref[0], n_blocks, tail, 0)

    @pl.when(i == 0)
    def _():
        zbuf[...] = jnp.zeros(zbuf.shape, F32)
        for_each_fill(lambda c: c.start())
        for_each_fill(lambda c: c.wait())

    def row_copy(r, d):
        return pltpu.make_async_copy(x_ref.at[pl.ds(r, 1), :], xs_ref.at[pl.ds(d, 1), :], sem)

    def issue(g, _):
        r0 = pl.multiple_of(g * DMA_UNROLL, DMA_UNROLL)
        for u in range(DMA_UNROLL):
            r = r0 + u
            row_copy(r, dest_ref[base + r]).start(priority=0)
            row_copy(r, dest_ref[n_tok + base + r]).start(priority=1)
        return 0

    lax.fori_loop(0, tm // DMA_UNROLL, issue, 0)

    for _ in range(2):
        pltpu.make_async_copy(x_ref, xs_ref.at[pl.ds(0, tm), :], sem).wait()


def _dispatch(dest_flat, fill_start, fill_len, meta, xn, n_rows):
    t = xn.shape[0]
    tm = DISPATCH_TILE
    return pl.pallas_call(
        _dispatch_kernel,
        grid_spec=pltpu.PrefetchScalarGridSpec(
            num_scalar_prefetch=4,
            grid=(t // tm,),
            in_specs=[pl.BlockSpec((tm, D_MODEL), lambda i, *_: (i, 0))],
            out_specs=pl.BlockSpec(memory_space=pl.ANY),
            scratch_shapes=[
                pltpu.VMEM((EXPERT_BLOCK, D_MODEL), F32),
                pltpu.SemaphoreType.DMA(()),
                pltpu.SemaphoreType.DMA(()),
            ],
        ),
        out_shape=jax.ShapeDtypeStruct((n_rows, D_MODEL), F32),
        compiler_params=pltpu.CompilerParams(dimension_semantics=("arbitrary",)),
        name="dispatch",
    )(dest_flat, fill_start, fill_len, meta, xn)


def _expert_kernel(be_ref, meta_ref, xs_ref, w1_ref, w3_ref, w2_ref, y_ref, w1b, w3b, w2b):
    b = pl.program_id(0)

    @pl.when(b >= meta_ref[0])
    def _():
        y_ref[...] = jnp.zeros(y_ref.shape, F32)

    @pl.when(b < meta_ref[0])
    def _():
        e = be_ref[b]
        prev = be_ref[jnp.maximum(b - 1, 0)]

        @pl.when((b == 0) | (e != prev))
        def _():
            w1b[...] = w1_ref[0, 0].astype(BF16)
            w3b[...] = w3_ref[0, 0].astype(BF16)
            w2b[...] = w2_ref[0, 0].astype(BF16)

        xb = xs_ref[...].astype(BF16)
        hid = jax.nn.silu(_dot(xb, w1b[...])) * _dot(xb, w3b[...])
        y_ref[...] = _dot(hid.astype(BF16), w2b[...])


def _experts(layer, block_e, meta, xs, w1, w3, w2):
    n_rows = xs.shape[0]
    bm = EXPERT_BLOCK

    def row_map(b, be, meta):
        return (jnp.minimum(b, meta[0] - 1), 0)

    def w_map(b, be, meta):
        return (layer, be[jnp.minimum(b, meta[0] - 1)], 0, 0)

    return pl.pallas_call(
        _expert_kernel,
        grid_spec=pltpu.PrefetchScalarGridSpec(
            num_scalar_prefetch=2,
            grid=(n_rows // bm,),
            in_specs=[
                pl.BlockSpec((bm, D_MODEL), row_map),
                pl.BlockSpec((1, 1, D_MODEL, D_EXPERT), w_map),
                pl.BlockSpec((1, 1, D_MODEL, D_EXPERT), w_map),
                pl.BlockSpec((1, 1, D_EXPERT, D_MODEL), w_map),
            ],
            out_specs=pl.BlockSpec((bm, D_MODEL), lambda b, be, meta: (b, 0)),
            scratch_shapes=[
                pltpu.VMEM((D_MODEL, D_EXPERT), BF16),
                pltpu.VMEM((D_MODEL, D_EXPERT), BF16),
                pltpu.VMEM((D_EXPERT, D_MODEL), BF16),
            ],
        ),
        out_shape=jax.ShapeDtypeStruct((n_rows, D_MODEL), F32),
        compiler_params=pltpu.CompilerParams(
            dimension_semantics=("arbitrary",), vmem_limit_bytes=VMEM_LIMIT),
        name="experts",
    )(block_e, meta, xs, w1, w3, w2)


def _combine_kernel(dest_ref, h_ref, wcol_ref, g_ref, y_ref, o_ref, buf, sem, *, final_norm):
    tc = h_ref.shape[0]
    i = pl.program_id(0)
    n = pl.num_programs(0)

    def row_copy(p, slot, k, r):
        return pltpu.make_async_copy(y_ref.at[pl.ds(p, 1), :],
                                     buf.at[2 * slot + k, pl.ds(r, 1), :], sem.at[slot])

    n_tok = n * tc

    def issue(tile, slot):
        def body(g, _):
            r0 = pl.multiple_of(g * DMA_UNROLL, DMA_UNROLL)
            for u in range(DMA_UNROLL):
                r = r0 + u
                tok = tile * tc + r
                row_copy(dest_ref[tok], slot, 0, r).start(priority=0)
                row_copy(dest_ref[n_tok + tok], slot, 1, r).start(priority=1)
            return 0
        lax.fori_loop(0, tc // DMA_UNROLL, body, 0)

    @pl.when(i == 0)
    def _():
        issue(0, 0)

    @pl.when(i + 1 < n)
    def _():
        issue(i + 1, (i + 1) % 2)

    slot = i % 2

    for k in range(2):
        pltpu.make_async_copy(y_ref.at[pl.ds(0, tc), :], buf.at[2 * slot + k], sem.at[slot]).wait()
    w = wcol_ref[...]
    out = h_ref[...] + (w[:, 0:1] * buf[2 * slot] + w[:, 1:2] * buf[2 * slot + 1])
    if final_norm:
        out = _rms(out, g_ref[...])
    o_ref[...] = out


def _combine(dest_flat, h, wcol, g, y, final_norm):
    t = h.shape[0]
    tc = COMBINE_TILE
    return pl.pallas_call(
        functools.partial(_combine_kernel, final_norm=final_norm),
        grid_spec=pltpu.PrefetchScalarGridSpec(
            num_scalar_prefetch=1,
            grid=(t // tc,),
            in_specs=[
                pl.BlockSpec((tc, D_MODEL), lambda i, d: (i, 0)),
                pl.BlockSpec((tc, ROUTER_ROWS), lambda i, d: (i, 0)),
                pl.BlockSpec((1, D_MODEL), lambda i, d: (0, 0)),
                pl.BlockSpec(memory_space=pl.ANY),
            ],
            out_specs=pl.BlockSpec((tc, D_MODEL), lambda i, d: (i, 0)),
            scratch_shapes=[
                pltpu.VMEM((4, tc, D_MODEL), F32),
                pltpu.SemaphoreType.DMA((2,)),
            ],
        ),
        out_shape=jax.ShapeDtypeStruct((t, D_MODEL), F32),
        compiler_params=pltpu.CompilerParams(
            dimension_semantics=("arbitrary",), vmem_limit_bytes=VMEM_LIMIT),
        name="combine",
    )(dest_flat, h, wcol, g, y)


def _block_diag(w):
    g, i, o = w.shape
    eye = jnp.eye(g, dtype=w.dtype)
    return jnp.einsum("gio,gk->giko", w, eye).reshape(g * i, g * o)


def _layer_params(l, a):
    row = lambda v: v.reshape(1, -1)
    vec_rows = [a["pool_b"][l].reshape(-1), a["pool_scale"][l], a["s5_d"][l], a["s5_glu_b"][l],
                a["lru_conv_b"][l], a["lru_ba"][l].reshape(-1), a["lru_bx"][l].reshape(-1),
                a["lru_lambda"][l],
                a["conv_a_w"][l][0], a["conv_a_w"][l][1], a["conv_a_w"][l][2],
                a["lru_conv_w"][l][0], a["lru_conv_w"][l][1], a["lru_conv_w"][l][2],
                a["lru_conv_w"][l][3], jnp.zeros((MIX_W,), F32)]
    s5v = jnp.stack([a["s5_lambda_re"][l].reshape(-1), a["s5_lambda_im"][l].reshape(-1),
                     jnp.repeat(a["s5_log_step"][l], S5_STATE)]
                    + [jnp.zeros((S5_LANES,), F32)] * 5)
    b_re = _block_diag(jnp.swapaxes(a["s5_b_re"][l], 1, 2))
    b_im = _block_diag(jnp.swapaxes(a["s5_b_im"][l], 1, 2))
    c_re = _block_diag(jnp.swapaxes(a["s5_c_re"][l], 1, 2))
    c_im = _block_diag(jnp.swapaxes(a["s5_c_im"][l], 1, 2))
    wr_t = jnp.zeros((ROUTER_ROWS, D_MODEL), F32)
    wr_t = wr_t.at[0:N_GROUPS].set(a["router_group_w"][l].T)
    wr_t = wr_t.at[SUBLANES:SUBLANES + N_EXPERTS].set(a["router_expert_w"][l].T)
    br = jnp.zeros((ROUTER_ROWS,), F32)
    br = br.at[0:N_GROUPS].set(a["router_group_b"][l])
    br = br.at[SUBLANES:SUBLANES + N_EXPERTS].set(a["router_expert_b"][l])
    return {
        "g1": row(a["norm1_g"][l]),
        "w_in": a["w_in"][l].astype(BF16),
        "vec": jnp.stack(vec_rows),
        "s5v": s5v,
        "wpool": _block_diag(a["pool_w"][l]).astype(BF16),
        "bbig": jnp.concatenate([b_re, b_im], axis=1).astype(BF16),
        "cbig": jnp.concatenate([c_re, -c_im], axis=0).astype(BF16),
        "glu_w": a["s5_glu_w"][l].astype(BF16),
        "wa": _block_diag(a["lru_wa"][l]).astype(BF16),
        "wx": _block_diag(a["lru_wx"][l]).astype(BF16),
        "wg": a["merge_gate_w"][l].astype(BF16),
        "bg": row(a["merge_gate_b"][l]),
        "bw": a["branch_w"][l].astype(BF16),
        "w_out": a["w_out"][l].astype(BF16),
        "g2": row(a["norm2_g"][l]),
        "wr_t": wr_t.astype(BF16),
        "br": br.reshape(ROUTER_ROWS, 1),
    }


def _moe(layer, h, p, w1, w3, w2, g_final, final_norm):
    t = h.shape[0]
    bm = EXPERT_BLOCK
    xn, route_i, wcol, counts = _router(h, p["g2"], p["wr_t"], p["br"])
    cnt = counts[:, 0]
    padded = ((cnt + bm - 1) // bm) * bm
    pad_end = jnp.cumsum(padded)
    pad_start = pad_end - padded
    n_rows = (-(-(2 * t) // bm)) * bm + N_EXPERTS * bm
    n_blocks = n_rows // bm
    block_row = jnp.arange(n_blocks, dtype=jnp.int32) * bm
    block_e = jnp.minimum(jnp.sum((pad_end[None, :] <= block_row[:, None]).astype(jnp.int32), axis=1),
                          N_EXPERTS - 1)
    meta = (pad_end[-1:] // bm).astype(jnp.int32)
    dest = _dest(route_i, pad_start.astype(F32).reshape(N_EXPERTS, 1))
    dest_flat = dest[0:2].reshape(-1)
    xs = _dispatch(dest_flat, (pad_start + cnt).astype(jnp.int32), (padded - cnt).astype(jnp.int32),
                   meta, xn, n_rows)
    y = _experts(layer, block_e, meta, xs, w1, w3, w2)
    return _combine(dest_flat, h, wcol, g_final, y, final_norm)


def kernel(x, norm1_g, w_in, conv_a_w, pool_w, pool_b, pool_scale, s5_lambda_re, s5_lambda_im, s5_log_step, s5_b_re, s5_b_im, s5_c_re, s5_c_im, s5_d, s5_glu_w, s5_glu_b, lru_conv_w, lru_conv_b, lru_wa, lru_ba, lru_wx, lru_bx, lru_lambda, merge_gate_w, merge_gate_b, branch_w, w_out, norm2_g, router_group_w, router_group_b, router_expert_w, router_expert_b, expert_w1, expert_w3, expert_w2, final_norm_g):
    a = dict(norm1_g=norm1_g, w_in=w_in, conv_a_w=conv_a_w, pool_w=pool_w, pool_b=pool_b,
             pool_scale=pool_scale, s5_lambda_re=s5_lambda_re, s5_lambda_im=s5_lambda_im,
             s5_log_step=s5_log_step, s5_b_re=s5_b_re, s5_b_im=s5_b_im, s5_c_re=s5_c_re,
             s5_c_im=s5_c_im, s5_d=s5_d, s5_glu_w=s5_glu_w, s5_glu_b=s5_glu_b,
             lru_conv_w=lru_conv_w, lru_conv_b=lru_conv_b, lru_wa=lru_wa, lru_ba=lru_ba,
             lru_wx=lru_wx, lru_bx=lru_bx, lru_lambda=lru_lambda, merge_gate_w=merge_gate_w,
             merge_gate_b=merge_gate_b, branch_w=branch_w, w_out=w_out, norm2_g=norm2_g,
             router_group_w=router_group_w, router_group_b=router_group_b,
             router_expert_w=router_expert_w, router_expert_b=router_expert_b)
    batch, seq, d = x.shape
    depth = norm1_g.shape[0]
    h = x.reshape(batch * seq, d)
    g_final = final_norm_g.reshape(1, d)
    for l in range(depth):
        p = _layer_params(l, a)
        h = _mixer(h, p, batch, seq)
        h = _moe(l, h, p, expert_w1, expert_w3, expert_w2, g_final, l == depth - 1)
    return h.reshape(batch, seq, d)
```

```python
import functools

import jax
import jax.numpy as jnp
from jax import lax
from jax.experimental import pallas as pl
from jax.experimental.pallas import tpu as pltpu

F32 = jnp.float32
BF16 = jnp.bfloat16

D_MODEL = 1024
MIX_W = 256
N_BRANCH = 4
POOL_WINDOWS = (2, 4, 8, 16)
POOL_GC = 64
S5_GROUPS = 16
S5_GROUP_CH = 16
S5_STATE = 64
S5_LANES = S5_GROUPS * S5_STATE
S5_EIG_CLIP = -1e-4
LRU_HEADS = 4
LRU_C = 8.0
IN_COLS = 1792
N_GROUPS = 4
EXP_PER_GROUP = 8
N_EXPERTS = 32
D_EXPERT = 512
EPS = 1e-6

SUBLANES = 8
LANES = 128
HALO = 16
SEQ_TILE = 256
ROUTE_TILE = 512
DISPATCH_TILE = 512
COMBINE_TILE = 256
EXPERT_BLOCK = 256
ROUTER_ROWS = 128
DEST_TILE = 2048
DMA_UNROLL = 8
PAD_PIECES = tuple(EXPERT_BLOCK >> (k + 1) for k in range(EXPERT_BLOCK.bit_length() - 1))
VMEM_LIMIT = 56 * 1024 * 1024

(_R_POOL_B, _R_POOL_SCALE, _R_S5_D, _R_GLU_B, _R_CONV_B, _R_BA, _R_BX, _R_LAM,
 _R_CA0, _R_CA1, _R_CA2, _R_CD0, _R_CD1, _R_CD2, _R_CD3) = range(15)


def _rms(x, g):
    return x * lax.rsqrt(jnp.mean(x * x, axis=-1, keepdims=True) + EPS) * g


def _dot(a, b):
    return jnp.dot(a, b, preferred_element_type=F32)


def _mixer_kernel(h_ref, g1_ref, win_ref, vec_ref, s5v_ref, wpool_ref, bbig_ref, cbig_ref,
                  glu_ref, wa_ref, wx_ref, wg_ref, bg_ref, bw_ref, wout_ref,
                  o_ref,
                  ext_a, ext_b, ext_d, st_ref, su_ref, xc_ref, yc_ref, lb_ref, s5c_ref, lruc_ref,
                  are_ref, aim_ref, alre_ref, alim_ref, wre_ref, wim_ref, ptre_ref, ptim_ref,
                  coef_ref):
    ts = h_ref.shape[0]
    n_pos = ts // SUBLANES
    s = pl.program_id(1)
    row8 = lax.broadcasted_iota(jnp.int32, (SUBLANES, S5_LANES), 0)

    def put(ref, val):
        for half in range(2):
            ref[half] = val[:, half * LANES:(half + 1) * LANES]

    def get(ref):
        return jnp.concatenate([ref[0], ref[1]], axis=1)

    def to_chunk_major(ref):
        return jnp.concatenate(
            [jnp.concatenate([ref[half, pl.ds(pos, SUBLANES, stride=n_pos), :] for pos in range(n_pos)],
                             axis=0) for half in range(2)], axis=1)

    def store_time_major(ref, pos, val):
        for half in range(2):
            ref[half, pl.ds(pos, SUBLANES, stride=n_pos), :] = val[:, half * LANES:(half + 1) * LANES]

    def prow(pos):
        return slice(pos * SUBLANES, (pos + 1) * SUBLANES)

    @pl.when(s == 0)
    def _start_of_sequence():
        zeros_halo = jnp.zeros((HALO, MIX_W), F32)
        ext_a[0:HALO, :] = zeros_halo
        ext_b[0:HALO, :] = zeros_halo
        ext_d[0:HALO, :] = zeros_halo
        s5c_ref[...] = jnp.zeros(s5c_ref.shape, F32)
        lruc_ref[...] = jnp.zeros(lruc_ref.shape, F32)

    @pl.when((s == 0) & (pl.program_id(0) == 0))
    def _s5_discretisation():
        lam_re = jnp.minimum(s5v_ref[0:1, :], S5_EIG_CLIP)
        lam_im = s5v_ref[1:2, :]
        dt = jnp.exp(s5v_ref[2:3, :])
        xr = lam_re * dt
        th = lam_im * dt

        def power(k):
            ek = jnp.exp(k * xr)
            return ek * jnp.cos(k * th), ek * jnp.sin(k * th)

        a_re, a_im = power(1.0)
        den = lam_re * lam_re + lam_im * lam_im
        coef_ref[0:1, :] = ((a_re - 1.0) * lam_re + a_im * lam_im) / den
        coef_ref[1:2, :] = (a_im * lam_re - (a_re - 1.0) * lam_im) / den
        are_ref[...] = jnp.broadcast_to(a_re, are_ref.shape)
        aim_ref[...] = jnp.broadcast_to(a_im, aim_ref.shape)
        al_re, al_im = power(float(n_pos))
        alre_ref[...] = jnp.broadcast_to(al_re, alre_ref.shape)
        alim_ref[...] = jnp.broadcast_to(al_im, alim_ref.shape)
        for j, sh in enumerate((1, 2, 4)):
            p_re, p_im = power(float(sh * n_pos))
            keep = row8 >= sh
            wre_ref[prow(j), :] = jnp.where(keep, p_re, 0.0)
            wim_ref[prow(j), :] = jnp.where(keep, p_im, 0.0)
        kk = (lax.broadcasted_iota(jnp.int32, (n_pos, S5_LANES), 0) + 1).astype(F32)
        t_re, t_im = power(kk)
        for pos in range(n_pos):
            ptre_ref[prow(pos), :] = jnp.broadcast_to(t_re[pos:pos + 1, :], (SUBLANES, S5_LANES))
            ptim_ref[prow(pos), :] = jnp.broadcast_to(t_im[pos:pos + 1, :], (SUBLANES, S5_LANES))

    def vrow(r):
        return vec_ref[r:r + 1, :]

    h = h_ref[...]
    xn = _rms(h, g1_ref[...])
    xnb = xn.astype(BF16)
    proj = _dot(xnb, win_ref[...])
    a_b = proj[:, 0:256]
    a_c = proj[:, 256:512]
    a_x = proj[:, 512:768]
    p_u = proj[:, 768:1024]
    s_u = proj[:, 1024:1280]
    l_x = proj[:, 1280:1536]
    l_g = proj[:, 1536:1792]

    ext_a[HALO:HALO + ts, :] = a_c * a_x
    conv = ext_a[HALO - 2:HALO - 2 + ts, :] * vrow(_R_CA0)
    conv = conv + ext_a[HALO - 1:HALO - 1 + ts, :] * vrow(_R_CA1)
    conv = conv + ext_a[HALO:HALO + ts, :] * vrow(_R_CA2)
    ya = a_b * conv
    ext_a[0:HALO, :] = ext_a[ts:ts + HALO, :]

    ext_b[HALO:HALO + ts, :] = p_u
    lane = lax.broadcasted_iota(jnp.int32, (1, MIX_W), 1)
    grp = jnp.right_shift(lane, 6)
    win = p_u
    acc = p_u
    sh = 1
    for gi, w in enumerate(POOL_WINDOWS):
        while sh < w:
            acc = acc + ext_b[HALO - sh:HALO - sh + ts, :]
            sh += 1
        if gi > 0:
            win = jnp.where(grp >= gi, acc, win)
        else:
            win = acc
    wlane = jnp.where(grp == 0, 2.0, jnp.where(grp == 1, 4.0, jnp.where(grp == 2, 8.0, 16.0)))
    tpos = (s * ts + lax.broadcasted_iota(jnp.int32, (ts, MIX_W), 0) + 1).astype(F32)
    cnt = jnp.minimum(tpos, wlane)
    pooled = win / cnt - p_u
    yb = (_dot(pooled.astype(BF16), wpool_ref[...]) + vrow(_R_POOL_B)) * vrow(_R_POOL_SCALE)
    ext_b[0:HALO, :] = ext_b[ts:ts + HALO, :]

    put(su_ref, s_u)
    u_cm = to_chunk_major(su_ref)
    bu = _dot(u_cm.astype(BF16), bbig_ref[...])
    bre = bu[:, :S5_LANES]
    bim = bu[:, S5_LANES:]
    c_re = coef_ref[0:1, :]
    c_im = coef_ref[1:2, :]
    st_ref[:, :S5_LANES] = c_re * bre - c_im * bim
    st_ref[:, S5_LANES:] = c_re * bim + c_im * bre
    a_re = are_ref[...]
    a_im = aim_ref[...]
    hr = st_ref[prow(0), :S5_LANES]
    hi = st_ref[prow(0), S5_LANES:]
    for pos in range(1, n_pos):
        hr, hi = (a_re * hr - a_im * hi + st_ref[prow(pos), :S5_LANES],
                  a_re * hi + a_im * hr + st_ref[prow(pos), S5_LANES:])
        st_ref[prow(pos), :S5_LANES] = hr
        st_ref[prow(pos), S5_LANES:] = hi
    first = row8 == 0
    fr = jnp.where(first, s5c_ref[0:1, :S5_LANES], pltpu.roll(hr, 1, axis=0))
    fi = jnp.where(first, s5c_ref[0:1, S5_LANES:], pltpu.roll(hi, 1, axis=0))
    for jj, shift in enumerate((1, 2, 4)):
        wr = wre_ref[prow(jj), :]
        wi = wim_ref[prow(jj), :]
        sr = pltpu.roll(fr, shift, axis=0)
        si = pltpu.roll(fi, shift, axis=0)
        fr, fi = fr + (wr * sr - wi * si), fi + (wr * si + wi * sr)
    al_re = alre_ref[...]
    al_im = alim_ref[...]
    nxt_re = al_re * fr - al_im * fi + hr
    nxt_im = al_re * fi + al_im * fr + hi
    s5c_ref[0:1, :S5_LANES] = nxt_re[SUBLANES - 1:SUBLANES, :]
    s5c_ref[0:1, S5_LANES:] = nxt_im[SUBLANES - 1:SUBLANES, :]
    for pos in range(n_pos):
        pr = ptre_ref[prow(pos), :]
        pi = ptim_ref[prow(pos), :]
        st_ref[prow(pos), :S5_LANES] = st_ref[prow(pos), :S5_LANES] + (pr * fr - pi * fi)
        st_ref[prow(pos), S5_LANES:] = st_ref[prow(pos), S5_LANES:] + (pr * fi + pi * fr)
    yc = (_dot(st_ref[:, :S5_LANES].astype(BF16), cbig_ref[:S5_LANES, :])
          + _dot(st_ref[:, S5_LANES:].astype(BF16), cbig_ref[S5_LANES:, :]))
    yc = yc + vrow(_R_S5_D) * u_cm
    yc = jax.nn.gelu(yc)
    yc = yc * jax.nn.sigmoid(_dot(yc.astype(BF16), glu_ref[...]) + vrow(_R_GLU_B))
    for pos in range(n_pos):
        store_time_major(yc_ref, pos, yc[prow(pos), :])
    yc = get(yc_ref)

    ext_d[HALO:HALO + ts, :] = l_x
    xc = ext_d[HALO - 3:HALO - 3 + ts, :] * vrow(_R_CD0)
    xc = xc + ext_d[HALO - 2:HALO - 2 + ts, :] * vrow(_R_CD1)
    xc = xc + ext_d[HALO - 1:HALO - 1 + ts, :] * vrow(_R_CD2)
    xc = xc + ext_d[HALO:HALO + ts, :] * vrow(_R_CD3)
    xc = xc + vrow(_R_CONV_B)
    ext_d[0:HALO, :] = ext_d[ts:ts + HALO, :]
    put(xc_ref, xc)
    xc = to_chunk_major(xc_ref)
    xcb = xc.astype(BF16)
    r_gate = jax.nn.sigmoid(_dot(xcb, wa_ref[...]) + vrow(_R_BA))
    i_gate = jax.nn.sigmoid(_dot(xcb, wx_ref[...]) + vrow(_R_BX))
    z = -vrow(_R_LAM)
    softplus = jnp.maximum(z, 0.0) + jnp.log1p(jnp.exp(-jnp.abs(z)))
    log_a = -LRU_C * r_gate * softplus
    a_t = jnp.exp(log_a)
    mult = jnp.sqrt(1.0 - a_t * a_t)
    b_t = mult * (i_gate * xc)
    hh = b_t[prow(0), :]
    aa = a_t[prow(0), :]
    h_loc = [hh]
    a_cum = [aa]
    for pos in range(1, n_pos):
        a_pos = a_t[prow(pos), :]
        hh = a_pos * hh + b_t[prow(pos), :]
        aa = a_pos * aa
        h_loc.append(hh)
        a_cum.append(aa)
    row8w = lax.broadcasted_iota(jnp.int32, (SUBLANES, MIX_W), 0)
    f = jnp.where(row8w == 0, lruc_ref[0:1, :], pltpu.roll(hh, 1, axis=0))
    m = pltpu.roll(aa, 1, axis=0)
    for shift in (1, 2, 4):
        keep = row8w >= shift
        f_s = pltpu.roll(f, shift, axis=0)
        m_s = pltpu.roll(m, shift, axis=0)
        f = jnp.where(keep, m * f_s + f, f)
        m = jnp.where(keep, m * m_s, m)
    lruc_ref[0:1, :] = (aa * f + hh)[SUBLANES - 1:SUBLANES, :]
    for pos in range(n_pos):
        store_time_major(lb_ref, pos, h_loc[pos] + a_cum[pos] * f)
    yd = get(lb_ref) * jax.nn.gelu(l_g)

    merged = None
    for k, yk in enumerate((ya, yb, yc, yd)):
        gate = jax.nn.sigmoid(_dot(xnb, wg_ref[:, k * D_MODEL:(k + 1) * D_MODEL])
                              + bg_ref[:, k * D_MODEL:(k + 1) * D_MODEL])
        term = gate * _dot(yk.astype(BF16), bw_ref[k])
        merged = term if merged is None else merged + term
    o_ref[...] = h + _dot(merged.astype(BF16), wout_ref[...])


def _const_spec(shape):
    nd = len(shape)
    return pl.BlockSpec(shape, lambda b, s: (0,) * nd)


def _mixer(h, p, batch, seq):
    ts = SEQ_TILE
    ns = seq // ts
    t = batch * seq
    weights = (p["g1"], p["w_in"], p["vec"], p["s5v"], p["wpool"], p["bbig"], p["cbig"], p["glu_w"],
               p["wa"], p["wx"], p["wg"], p["bg"], p["bw"], p["w_out"])
    in_specs = [pl.BlockSpec((ts, D_MODEL), lambda b, s: (b * ns + s, 0))]
    in_specs += [_const_spec(w.shape) for w in weights]
    return pl.pallas_call(
        _mixer_kernel,
        grid=(batch, ns),
        in_specs=in_specs,
        out_specs=pl.BlockSpec((ts, D_MODEL), lambda b, s: (b * ns + s, 0)),
        out_shape=jax.ShapeDtypeStruct((t, D_MODEL), F32),
        scratch_shapes=[
            pltpu.VMEM((HALO + ts, MIX_W), F32),
            pltpu.VMEM((HALO + ts, MIX_W), F32),
            pltpu.VMEM((HALO + ts, MIX_W), F32),
            pltpu.VMEM((ts, 2 * S5_LANES), F32),
            pltpu.VMEM((2, ts, LANES), F32),
            pltpu.VMEM((2, ts, LANES), F32),
            pltpu.VMEM((2, ts, LANES), F32),
            pltpu.VMEM((2, ts, LANES), F32),
            pltpu.VMEM((SUBLANES, 2 * S5_LANES), F32),
            pltpu.VMEM((SUBLANES, MIX_W), F32),
            pltpu.VMEM((SUBLANES, S5_LANES), F32),
            pltpu.VMEM((SUBLANES, S5_LANES), F32),
            pltpu.VMEM((SUBLANES, S5_LANES), F32),
            pltpu.VMEM((SUBLANES, S5_LANES), F32),
            pltpu.VMEM((3 * SUBLANES, S5_LANES), F32),
            pltpu.VMEM((3 * SUBLANES, S5_LANES), F32),
            pltpu.VMEM((ts, S5_LANES), F32),
            pltpu.VMEM((ts, S5_LANES), F32),
            pltpu.VMEM((SUBLANES, S5_LANES), F32),
        ],
        compiler_params=pltpu.CompilerParams(
            dimension_semantics=("arbitrary", "arbitrary"),
            vmem_limit_bytes=VMEM_LIMIT),
        name="mixer",
    )(h, *weights)


def _router_kernel(h_ref, g2_ref, wr_ref, br_ref, xn_ref, ri_ref, wcol_ref, cnt_ref, carry_ref):
    tm = h_ref.shape[0]
    i = pl.program_id(0)

    @pl.when(i == 0)
    def _():
        carry_ref[...] = jnp.zeros(carry_ref.shape, F32)

    xn = _rms(h_ref[...], g2_ref[...])
    xn_ref[...] = xn
    logits = lax.dot_general(wr_ref[...], xn.astype(BF16), (((1,), (1,)), ((), ())),
                             preferred_element_type=F32) + br_ref[...]
    row8 = lax.broadcasted_iota(jnp.int32, (SUBLANES, tm), 0)
    neg_inf = jnp.float32(-jnp.inf)
    gl = jnp.where(row8 < N_GROUPS, logits[0:SUBLANES, :], neg_inf)
    gmax = jnp.max(gl, axis=0, keepdims=True)
    ge = jnp.exp(gl - gmax)
    gp = ge / jnp.sum(ge, axis=0, keepdims=True)
    g_val = jnp.max(gp, axis=0, keepdims=True)
    g_idx = jnp.min(jnp.where(gp == g_val, row8, SUBLANES), axis=0, keepdims=True)
    sel = logits[4 * SUBLANES:5 * SUBLANES, :]
    for g in (2, 1, 0):
        sel = jnp.where(g_idx == g, logits[(g + 1) * SUBLANES:(g + 2) * SUBLANES, :], sel)
    v1 = jnp.max(sel, axis=0, keepdims=True)
    i1 = jnp.min(jnp.where(sel == v1, row8, SUBLANES), axis=0, keepdims=True)
    sel2 = jnp.where(row8 == i1, neg_inf, sel)
    v2 = jnp.max(sel2, axis=0, keepdims=True)
    i2 = jnp.min(jnp.where(sel2 == v2, row8, SUBLANES), axis=0, keepdims=True)
    e2 = jnp.exp(v2 - v1)
    denom = 1.0 + e2
    w1 = (1.0 / denom) * g_val
    w2 = (e2 / denom) * g_val
    eid0 = g_idx * EXP_PER_GROUP + i1
    eid1 = g_idx * EXP_PER_GROUP + i2
    e32 = lax.broadcasted_iota(jnp.int32, (N_EXPERTS, tm), 0)
    oh0 = (e32 == eid0).astype(F32)
    oh1 = (e32 == eid1).astype(F32)
    oh = oh0 + oh1
    before = (lax.broadcasted_iota(jnp.int32, (tm, tm), 0)
              < lax.broadcasted_iota(jnp.int32, (tm, tm), 1)).astype(BF16)
    base = _dot(oh.astype(BF16), before) + carry_ref[:, 0:1]
    rank0 = jnp.sum(oh0 * base, axis=0, keepdims=True)
    rank1 = jnp.sum(oh1 * base, axis=0, keepdims=True)
    new_carry = carry_ref[...] + jnp.sum(oh, axis=1, keepdims=True)
    carry_ref[...] = new_carry
    cnt_ref[...] = new_carry.astype(jnp.int32)
    ri_ref[...] = jnp.zeros(ri_ref.shape, jnp.int32)
    ri_ref[0:1, :] = eid0
    ri_ref[1:2, :] = eid1
    ri_ref[2:3, :] = rank0.astype(jnp.int32)
    ri_ref[3:4, :] = rank1.astype(jnp.int32)
    rows = lax.broadcasted_iota(jnp.int32, (ROUTER_ROWS, tm), 0)
    wrows = jnp.where(rows == 0, w1, jnp.where(rows == 1, w2, 0.0))
    wcol_ref[...] = wrows.T


def _router(h, g2, wr_t, br_col):
    t = h.shape[0]
    tm = ROUTE_TILE
    return pl.pallas_call(
        _router_kernel,
        grid=(t // tm,),
        in_specs=[
            pl.BlockSpec((tm, D_MODEL), lambda i: (i, 0)),
            pl.BlockSpec((1, D_MODEL), lambda i: (0, 0)),
            pl.BlockSpec((ROUTER_ROWS, D_MODEL), lambda i: (0, 0)),
            pl.BlockSpec((ROUTER_ROWS, 1), lambda i: (0, 0)),
        ],
        out_specs=[
            pl.BlockSpec((tm, D_MODEL), lambda i: (i, 0)),
            pl.BlockSpec((SUBLANES, tm), lambda i: (0, i)),
            pl.BlockSpec((tm, ROUTER_ROWS), lambda i: (i, 0)),
            pl.BlockSpec((N_EXPERTS, 128), lambda i: (0, 0)),
        ],
        out_shape=[
            jax.ShapeDtypeStruct((t, D_MODEL), F32),
            jax.ShapeDtypeStruct((SUBLANES, t), jnp.int32),
            jax.ShapeDtypeStruct((t, ROUTER_ROWS), F32),
            jax.ShapeDtypeStruct((N_EXPERTS, 128), jnp.int32),
        ],
        scratch_shapes=[pltpu.VMEM((N_EXPERTS, 128), F32)],
        compiler_params=pltpu.CompilerParams(
            dimension_semantics=("arbitrary",), vmem_limit_bytes=VMEM_LIMIT),
        name="router",
    )(h, g2, wr_t, br_col)


def _dest_kernel(ri_ref, ps_ref, o_ref):
    tm = ri_ref.shape[1]
    e32 = lax.broadcasted_iota(jnp.int32, (N_EXPERTS, tm), 0)
    o_ref[...] = jnp.zeros(o_ref.shape, jnp.int32)
    for k in range(2):
        start = jnp.sum(jnp.where(e32 == ri_ref[k:k + 1, :], ps_ref[...], 0.0), axis=0, keepdims=True)
        o_ref[k:k + 1, :] = start.astype(jnp.int32) + ri_ref[2 + k:3 + k, :]


def _dest(route_i, pad_start_col):
    t = route_i.shape[1]
    tm = DEST_TILE
    return pl.pallas_call(
        _dest_kernel,
        grid=(t // tm,),
        in_specs=[
            pl.BlockSpec((SUBLANES, tm), lambda i: (0, i)),
            pl.BlockSpec((N_EXPERTS, 1), lambda i: (0, 0)),
        ],
        out_specs=pl.BlockSpec((SUBLANES, tm), lambda i: (0, i)),
        out_shape=jax.ShapeDtypeStruct((SUBLANES, t), jnp.int32),
        compiler_params=pltpu.CompilerParams(dimension_semantics=("arbitrary",)),
        name="dest",
    )(route_i, pad_start_col)


def _dispatch_kernel(dest_ref, fs_ref, fl_ref, meta_ref, x_ref, xs_ref, zbuf, sem, zsem):
    tm = x_ref.shape[0]
    bm = zbuf.shape[0]
    n_blocks = xs_ref.shape[0] // bm
    i = pl.program_id(0)
    n_tok = pl.num_programs(0) * tm
    base = i * tm

    def pad_copy(off, p):
        return pltpu.make_async_copy(zbuf.at[pl.ds(0, p), :], xs_ref.at[pl.ds(off, p), :], zsem)

    def for_each_fill(act):
        def segment(e, _):
            off = fs_ref[e]
            n = fl_ref[e]
            head = n & (SUBLANES - 1)
            for j in range(SUBLANES - 1):
                @pl.when(j < head)
                def _(off=off, j=j):
                    act(pad_copy(off + j, 1))

            off = pl.multiple_of(off + head, SUBLANES)
            for p in PAD_PIECES:
                if p < SUBLANES:
                    continue
                piece = n & p

                @pl.when(piece != 0)
                def _(off=off, p=p):
                    act(pad_copy(off, p))

                off = pl.multiple_of(off + piece, SUBLANES)
            return 0

        lax.fori_loop(0, N_EXPERTS, segment, 0)

        def tail(b, _):
            act(pad_copy(pl.multiple_of(b * bm, bm), bm))
            return 0

        lax.fori_loop(meta_ref[0], n_blocks, tail, 0)

    @pl.when(i == 0)
    def _():
        zbuf[...] = jnp.zeros(zbuf.shape, F32)
        for_each_fill(lambda c: c.start())
        for_each_fill(lambda c: c.wait())

    def row_copy(r, d):
        return pltpu.make_async_copy(x_ref.at[pl.ds(r, 1), :], xs_ref.at[pl.ds(d, 1), :], sem)

    def issue(g, _):
        r0 = pl.multiple_of(g * DMA_UNROLL, DMA_UNROLL)
        for u in range(DMA_UNROLL):
            r = r0 + u
            row_copy(r, dest_ref[base + r]).start(priority=0)
            row_copy(r, dest_ref[n_tok + base + r]).start(priority=1)
        return 0

    lax.fori_loop(0, tm // DMA_UNROLL, issue, 0)

    for _ in range(2):
        pltpu.make_async_copy(x_ref, xs_ref.at[pl.ds(0, tm), :], sem).wait()


def _dispatch(dest_flat, fill_start, fill_len, meta, xn, n_rows):
    t = xn.shape[0]
    tm = DISPATCH_TILE
    return pl.pallas_call(
        _dispatch_kernel,
        grid_spec=pltpu.PrefetchScalarGridSpec(
            num_scalar_prefetch=4,
            grid=(t // tm,),
            in_specs=[pl.BlockSpec((tm, D_MODEL), lambda i, *_: (i, 0))],
            out_specs=pl.BlockSpec(memory_space=pl.ANY),
            scratch_shapes=[
                pltpu.VMEM((EXPERT_BLOCK, D_MODEL), F32),
                pltpu.SemaphoreType.DMA(()),
                pltpu.SemaphoreType.DMA(()),
            ],
        ),
        out_shape=jax.ShapeDtypeStruct((n_rows, D_MODEL), F32),
        compiler_params=pltpu.CompilerParams(dimension_semantics=("arbitrary",)),
        name="dispatch",
    )(dest_flat, fill_start, fill_len, meta, xn)


def _expert_kernel(ss_ref, nc_ref, meta_ref, xs_ref, w1_ref, w3_ref, w2_ref, y_ref,
                   xbuf, ybuf, w1b, w3b, w2b, insem, outsem):
    e = pl.program_id(0)
    ch = xbuf.shape[1]
    n_blocks = y_ref.shape[0] // ch
    start = ss_ref[e]
    nchunk = nc_ref[e]

    def rows(c):
        return pl.ds(pl.multiple_of(start + c * ch, ch), ch)

    def in_copy(c, slot):
        return pltpu.make_async_copy(xs_ref.at[rows(c), :], xbuf.at[slot], insem.at[slot])

    def out_copy(c, slot):
        return pltpu.make_async_copy(ybuf.at[slot], y_ref.at[rows(c), :], outsem.at[slot])

    @pl.when(nchunk > 0)
    def _():
        in_copy(0, 0).start()

    w1b[...] = w1_ref[0, 0].astype(BF16)
    w3b[...] = w3_ref[0, 0].astype(BF16)
    w2b[...] = w2_ref[0, 0].astype(BF16)

    def body(c, _):
        slot = c % 2

        @pl.when(c + 1 < nchunk)
        def _():
            in_copy(c + 1, 1 - slot).start()

        in_copy(c, slot).wait()

        @pl.when(c >= 2)
        def _():
            out_copy(c - 2, slot).wait()

        xb = xbuf[slot].astype(BF16)
        hid = jax.nn.silu(_dot(xb, w1b[...])) * _dot(xb, w3b[...])
        ybuf[slot] = _dot(hid.astype(BF16), w2b[...])
        out_copy(c, slot).start()
        return 0

    lax.fori_loop(0, nchunk, body, 0)

    @pl.when(nchunk >= 2)
    def _():
        out_copy(nchunk - 2, nchunk % 2).wait()

    @pl.when(nchunk >= 1)
    def _():
        out_copy(nchunk - 1, (nchunk - 1) % 2).wait()

    @pl.when(e == pl.num_programs(0) - 1)
    def _():
        ybuf[0] = jnp.zeros(ybuf.shape[1:], F32)

        def tail_copy(b):
            return pltpu.make_async_copy(
                ybuf.at[0], y_ref.at[pl.ds(pl.multiple_of(b * ch, ch), ch), :], outsem.at[0])

        def tail_start(b, _):
            tail_copy(b).start()
            return 0

        def tail_wait(b, _):
            tail_copy(b).wait()
            return 0

        lax.fori_loop(meta_ref[0], n_blocks, tail_start, 0)
        lax.fori_loop(meta_ref[0], n_blocks, tail_wait, 0)


def _experts(layer, seg_start, seg_chunks, meta, xs, w1, w3, w2):
    n_rows = xs.shape[0]
    ch = EXPERT_BLOCK

    def w_map(e, *_):
        return (layer, e, 0, 0)

    return pl.pallas_call(
        _expert_kernel,
        grid_spec=pltpu.PrefetchScalarGridSpec(
            num_scalar_prefetch=3,
            grid=(N_EXPERTS,),
            in_specs=[
                pl.BlockSpec(memory_space=pl.ANY),
                pl.BlockSpec((1, 1, D_MODEL, D_EXPERT), w_map),
                pl.BlockSpec((1, 1, D_MODEL, D_EXPERT), w_map),
                pl.BlockSpec((1, 1, D_EXPERT, D_MODEL), w_map),
            ],
            out_specs=pl.BlockSpec(memory_space=pl.ANY),
            scratch_shapes=[
                pltpu.VMEM((2, ch, D_MODEL), F32),
                pltpu.VMEM((2, ch, D_MODEL), F32),
                pltpu.VMEM((D_MODEL, D_EXPERT), BF16),
                pltpu.VMEM((D_MODEL, D_EXPERT), BF16),
                pltpu.VMEM((D_EXPERT, D_MODEL), BF16),
                pltpu.SemaphoreType.DMA((2,)),
                pltpu.SemaphoreType.DMA((2,)),
            ],
        ),
        out_shape=jax.ShapeDtypeStruct((n_rows, D_MODEL), F32),
        compiler_params=pltpu.CompilerParams(
            dimension_semantics=("arbitrary",), vmem_limit_bytes=VMEM_LIMIT),
        name="experts",
    )(seg_start, seg_chunks, meta, xs, w1, w3, w2)


def _combine_kernel(dest_ref, h_ref, wcol_ref, g_ref, y_ref, o_ref, buf, sem, *, final_norm):
    tc = h_ref.shape[0]
    i = pl.program_id(0)
    n = pl.num_programs(0)

    def row_copy(p, slot, k, r):
        return pltpu.make_async_copy(y_ref.at[pl.ds(p, 1), :],
                                     buf.at[2 * slot + k, pl.ds(r, 1), :], sem.at[slot])

    n_tok = n * tc

    def issue(tile, slot):
        def body(g, _):
            r0 = pl.multiple_of(g * DMA_UNROLL, DMA_UNROLL)
            for u in range(DMA_UNROLL):
                r = r0 + u
                tok = tile * tc + r
                row_copy(dest_ref[tok], slot, 0, r).start(priority=0)
                row_copy(dest_ref[n_tok + tok], slot, 1, r).start(priority=1)
            return 0
        lax.fori_loop(0, tc // DMA_UNROLL, body, 0)

    @pl.when(i == 0)
    def _():
        issue(0, 0)

    @pl.when(i + 1 < n)
    def _():
        issue(i + 1, (i + 1) % 2)

    slot = i % 2

    for k in range(2):
        pltpu.make_async_copy(y_ref.at[pl.ds(0, tc), :], buf.at[2 * slot + k], sem.at[slot]).wait()
    w = wcol_ref[...]
    out = h_ref[...] + (w[:, 0:1] * buf[2 * slot] + w[:, 1:2] * buf[2 * slot + 1])
    if final_norm:
        out = _rms(out, g_ref[...])
    o_ref[...] = out


def _combine(dest_flat, h, wcol, g, y, final_norm):
    t = h.shape[0]
    tc = COMBINE_TILE
    return pl.pallas_call(
        functools.partial(_combine_kernel, final_norm=final_norm),
        grid_spec=pltpu.PrefetchScalarGridSpec(
            num_scalar_prefetch=1,
            grid=(t // tc,),
            in_specs=[
                pl.BlockSpec((tc, D_MODEL), lambda i, d: (i, 0)),
                pl.BlockSpec((tc, ROUTER_ROWS), lambda i, d: (i, 0)),
                pl.BlockSpec((1, D_MODEL), lambda i, d: (0, 0)),
                pl.BlockSpec(memory_space=pl.ANY),
            ],
            out_specs=pl.BlockSpec((tc, D_MODEL), lambda i, d: (i, 0)),
            scratch_shapes=[
                pltpu.VMEM((4, tc, D_MODEL), F32),
                pltpu.SemaphoreType.DMA((2,)),
            ],
        ),
        out_shape=jax.ShapeDtypeStruct((t, D_MODEL), F32),
        compiler_params=pltpu.CompilerParams(
            dimension_semantics=("arbitrary",), vmem_limit_bytes=VMEM_LIMIT),
        name="combine",
    )(dest_flat, h, wcol, g, y)


def _block_diag(w):
    g, i, o = w.shape
    eye = jnp.eye(g, dtype=w.dtype)
    return jnp.einsum("gio,gk->giko", w, eye).reshape(g * i, g * o)


def _layer_params(l, a):
    row = lambda v: v.reshape(1, -1)
    vec_rows = [a["pool_b"][l].reshape(-1), a["pool_scale"][l], a["s5_d"][l], a["s5_glu_b"][l],
                a["lru_conv_b"][l], a["lru_ba"][l].reshape(-1), a["lru_bx"][l].reshape(-1),
                a["lru_lambda"][l],
                a["conv_a_w"][l][0], a["conv_a_w"][l][1], a["conv_a_w"][l][2],
                a["lru_conv_w"][l][0], a["lru_conv_w"][l][1], a["lru_conv_w"][l][2],
                a["lru_conv_w"][l][3], jnp.zeros((MIX_W,), F32)]
    s5v = jnp.stack([a["s5_lambda_re"][l].reshape(-1), a["s5_lambda_im"][l].reshape(-1),
                     jnp.repeat(a["s5_log_step"][l], S5_STATE)]
                    + [jnp.zeros((S5_LANES,), F32)] * 5)
    b_re = _block_diag(jnp.swapaxes(a["s5_b_re"][l], 1, 2))
    b_im = _block_diag(jnp.swapaxes(a["s5_b_im"][l], 1, 2))
    c_re = _block_diag(jnp.swapaxes(a["s5_c_re"][l], 1, 2))
    c_im = _block_diag(jnp.swapaxes(a["s5_c_im"][l], 1, 2))
    wr_t = jnp.zeros((ROUTER_ROWS, D_MODEL), F32)
    wr_t = wr_t.at[0:N_GROUPS].set(a["router_group_w"][l].T)
    wr_t = wr_t.at[SUBLANES:SUBLANES + N_EXPERTS].set(a["router_expert_w"][l].T)
    br = jnp.zeros((ROUTER_ROWS,), F32)
    br = br.at[0:N_GROUPS].set(a["router_group_b"][l])
    br = br.at[SUBLANES:SUBLANES + N_EXPERTS].set(a["router_expert_b"][l])
    return {
        "g1": row(a["norm1_g"][l]),
        "w_in": a["w_in"][l].astype(BF16),
        "vec": jnp.stack(vec_rows),
        "s5v": s5v,
        "wpool": _block_diag(a["pool_w"][l]).astype(BF16),
        "bbig": jnp.concatenate([b_re, b_im], axis=1).astype(BF16),
        "cbig": jnp.concatenate([c_re, -c_im], axis=0).astype(BF16),
        "glu_w": a["s5_glu_w"][l].astype(BF16),
        "wa": _block_diag(a["lru_wa"][l]).astype(BF16),
        "wx": _block_diag(a["lru_wx"][l]).astype(BF16),
        "wg": a["merge_gate_w"][l].astype(BF16),
        "bg": row(a["merge_gate_b"][l]),
        "bw": a["branch_w"][l].astype(BF16),
        "w_out": a["w_out"][l].astype(BF16),
        "g2": row(a["norm2_g"][l]),
        "wr_t": wr_t.astype(BF16),
        "br": br.reshape(ROUTER_ROWS, 1),
    }


def _moe(layer, h, p, w1, w3, w2, g_final, final_norm):
    t = h.shape[0]
    bm = EXPERT_BLOCK
    xn, route_i, wcol, counts = _router(h, p["g2"], p["wr_t"], p["br"])
    cnt = counts[:, 0]
    padded = ((cnt + bm - 1) // bm) * bm
    pad_end = jnp.cumsum(padded)
    pad_start = pad_end - padded
    n_rows = (-(-(2 * t) // bm)) * bm + N_EXPERTS * bm
    n_blocks = n_rows // bm
    meta = (pad_end[-1:] // bm).astype(jnp.int32)
    dest = _dest(route_i, pad_start.astype(F32).reshape(N_EXPERTS, 1))
    dest_flat = dest[0:2].reshape(-1)
    xs = _dispatch(dest_flat, (pad_start + cnt).astype(jnp.int32), (padded - cnt).astype(jnp.int32),
                   meta, xn, n_rows)
    y = _experts(layer, pad_start.astype(jnp.int32), (padded // bm).astype(jnp.int32), meta,
                 xs, w1, w3, w2)
    return _combine(dest_flat, h, wcol, g_final, y, final_norm)


def kernel(x, norm1_g, w_in, conv_a_w, pool_w, pool_b, pool_scale, s5_lambda_re, s5_lambda_im, s5_log_step, s5_b_re, s5_b_im, s5_c_re, s5_c_im, s5_d, s5_glu_w, s5_glu_b, lru_conv_w, lru_conv_b, lru_wa, lru_ba, lru_wx, lru_bx, lru_lambda, merge_gate_w, merge_gate_b, branch_w, w_out, norm2_g, router_group_w, router_group_b, router_expert_w, router_expert_b, expert_w1, expert_w3, expert_w2, final_norm_g):
    a = dict(norm1_g=norm1_g, w_in=w_in, conv_a_w=conv_a_w, pool_w=pool_w, pool_b=pool_b,
             pool_scale=pool_scale, s5_lambda_re=s5_lambda_re, s5_lambda_im=s5_lambda_im,
             s5_log_step=s5_log_step, s5_b_re=s5_b_re, s5_b_im=s5_b_im, s5_c_re=s5_c_re,
             s5_c_im=s5_c_im, s5_d=s5_d, s5_glu_w=s5_glu_w, s5_glu_b=s5_glu_b,
             lru_conv_w=lru_conv_w, lru_conv_b=lru_conv_b, lru_wa=lru_wa, lru_ba=lru_ba,
             lru_wx=lru_wx, lru_bx=lru_bx, lru_lambda=lru_lambda, merge_gate_w=merge_gate_w,
             merge_gate_b=merge_gate_b, branch_w=branch_w, w_out=w_out, norm2_g=norm2_g,
             router_group_w=router_group_w, router_group_b=router_group_b,
             router_expert_w=router_expert_w, router_expert_b=router_expert_b)
    batch, seq, d = x.shape
    depth = norm1_g.shape[0]
    h = x.reshape(batch * seq, d)
    g_final = final_norm_g.reshape(1, d)
    for l in range(depth):
        p = _layer_params(l, a)
        h = _mixer(h, p, batch, seq)
        h = _moe(l, h, p, expert_w1, expert_w3, expert_w2, g_final, l == depth - 1)
    return h.reshape(batch, seq, d)
```

```python
import functools

import jax
import jax.numpy as jnp
from jax import lax
from jax.experimental import pallas as pl
from jax.experimental.pallas import tpu as pltpu

F32 = jnp.float32
BF16 = jnp.bfloat16

D_MODEL = 1024
MIX_W = 256
N_BRANCH = 4
POOL_WINDOWS = (2, 4, 8, 16)
POOL_GC = 64
S5_GROUPS = 16
S5_GROUP_CH = 16
S5_STATE = 64
S5_LANES = S5_GROUPS * S5_STATE
S5_EIG_CLIP = -1e-4
LRU_HEADS = 4
LRU_C = 8.0
IN_COLS = 1792
N_GROUPS = 4
EXP_PER_GROUP = 8
N_EXPERTS = 32
D_EXPERT = 512
EPS = 1e-6

SUBLANES = 8
LANES = 128
ROW_WORDS = D_MODEL // 2
HALO = 16
SEQ_TILE = 256
ROUTE_TILE = 512
DISPATCH_TILE = 512
COMBINE_TILE = 256
EXPERT_BLOCK = 256
ROUTER_ROWS = 128
DEST_TILE = 2048
DMA_UNROLL = 8
PAD_PIECES = tuple(EXPERT_BLOCK >> (k + 1) for k in range(EXPERT_BLOCK.bit_length() - 1))
VMEM_LIMIT = 56 * 1024 * 1024

(_R_POOL_B, _R_POOL_SCALE, _R_S5_D, _R_GLU_B, _R_CONV_B, _R_BA, _R_BX, _R_LAM,
 _R_CA0, _R_CA1, _R_CA2, _R_CD0, _R_CD1, _R_CD2, _R_CD3) = range(15)


def _rms(x, g):
    return x * lax.rsqrt(jnp.mean(x * x, axis=-1, keepdims=True) + EPS) * g


def _dot(a, b):
    return jnp.dot(a, b, preferred_element_type=F32)


def _pack_rows(x):
    half = x.shape[1] // 2
    lo = lax.bitcast_convert_type(x[:, :half].astype(BF16).astype(F32), jnp.uint32)
    hi = lax.bitcast_convert_type(x[:, half:].astype(BF16).astype(F32), jnp.uint32)
    return (hi & jnp.uint32(0xFFFF0000)) | (lo >> 16)


def _unpack_rows(w):
    lo = lax.bitcast_convert_type(w << 16, F32)
    hi = lax.bitcast_convert_type(w & jnp.uint32(0xFFFF0000), F32)
    return lo, hi


def _mixer_kernel(h_ref, g1_ref, win_ref, vec_ref, s5v_ref, wpool_ref, bbig_ref, cbig_ref,
                  glu_ref, wa_ref, wx_ref, wg_ref, bg_ref, bw_ref, wout_ref,
                  o_ref,
                  ext_a, ext_b, ext_d, st_ref, su_ref, xc_ref, yc_ref, lb_ref, s5c_ref, lruc_ref,
                  are_ref, aim_ref, alre_ref, alim_ref, wre_ref, wim_ref, ptre_ref, ptim_ref,
                  coef_ref):
    ts = h_ref.shape[0]
    n_pos = ts // SUBLANES
    s = pl.program_id(1)
    row8 = lax.broadcasted_iota(jnp.int32, (SUBLANES, S5_LANES), 0)

    def put(ref, val):
        for half in range(2):
            ref[half] = val[:, half * LANES:(half + 1) * LANES]

    def get(ref):
        return jnp.concatenate([ref[0], ref[1]], axis=1)

    def to_chunk_major(ref):
        return jnp.concatenate(
            [jnp.concatenate([ref[half, pl.ds(pos, SUBLANES, stride=n_pos), :] for pos in range(n_pos)],
                             axis=0) for half in range(2)], axis=1)

    def store_time_major(ref, pos, val):
        for half in range(2):
            ref[half, pl.ds(pos, SUBLANES, stride=n_pos), :] = val[:, half * LANES:(half + 1) * LANES]

    def prow(pos):
        return slice(pos * SUBLANES, (pos + 1) * SUBLANES)

    @pl.when(s == 0)
    def _start_of_sequence():
        zeros_halo = jnp.zeros((HALO, MIX_W), F32)
        ext_a[0:HALO, :] = zeros_halo
        ext_b[0:HALO, :] = zeros_halo
        ext_d[0:HALO, :] = zeros_halo
        s5c_ref[...] = jnp.zeros(s5c_ref.shape, F32)
        lruc_ref[...] = jnp.zeros(lruc_ref.shape, F32)

    @pl.when((s == 0) & (pl.program_id(0) == 0))
    def _s5_discretisation():
        lam_re = jnp.minimum(s5v_ref[0:1, :], S5_EIG_CLIP)
        lam_im = s5v_ref[1:2, :]
        dt = jnp.exp(s5v_ref[2:3, :])
        xr = lam_re * dt
        th = lam_im * dt

        def power(k):
            ek = jnp.exp(k * xr)
            return ek * jnp.cos(k * th), ek * jnp.sin(k * th)

        a_re, a_im = power(1.0)
        den = lam_re * lam_re + lam_im * lam_im
        coef_ref[0:1, :] = ((a_re - 1.0) * lam_re + a_im * lam_im) / den
        coef_ref[1:2, :] = (a_im * lam_re - (a_re - 1.0) * lam_im) / den
        are_ref[...] = jnp.broadcast_to(a_re, are_ref.shape)
        aim_ref[...] = jnp.broadcast_to(a_im, aim_ref.shape)
        al_re, al_im = power(float(n_pos))
        alre_ref[...] = jnp.broadcast_to(al_re, alre_ref.shape)
        alim_ref[...] = jnp.broadcast_to(al_im, alim_ref.shape)
        for j, sh in enumerate((1, 2, 4)):
            p_re, p_im = power(float(sh * n_pos))
            keep = row8 >= sh
            wre_ref[prow(j), :] = jnp.where(keep, p_re, 0.0)
            wim_ref[prow(j), :] = jnp.where(keep, p_im, 0.0)
        kk = (lax.broadcasted_iota(jnp.int32, (n_pos, S5_LANES), 0) + 1).astype(F32)
        t_re, t_im = power(kk)
        for pos in range(n_pos):
            ptre_ref[prow(pos), :] = jnp.broadcast_to(t_re[pos:pos + 1, :], (SUBLANES, S5_LANES))
            ptim_ref[prow(pos), :] = jnp.broadcast_to(t_im[pos:pos + 1, :], (SUBLANES, S5_LANES))

    def vrow(r):
        return vec_ref[r:r + 1, :]

    h = h_ref[...]
    xn = _rms(h, g1_ref[...])
    xnb = xn.astype(BF16)
    proj = _dot(xnb, win_ref[...])
    a_b = proj[:, 0:256]
    a_c = proj[:, 256:512]
    a_x = proj[:, 512:768]
    p_u = proj[:, 768:1024]
    s_u = proj[:, 1024:1280]
    l_x = proj[:, 1280:1536]
    l_g = proj[:, 1536:1792]

    ext_a[HALO:HALO + ts, :] = a_c * a_x
    conv = ext_a[HALO - 2:HALO - 2 + ts, :] * vrow(_R_CA0)
    conv = conv + ext_a[HALO - 1:HALO - 1 + ts, :] * vrow(_R_CA1)
    conv = conv + ext_a[HALO:HALO + ts, :] * vrow(_R_CA2)
    ya = a_b * conv
    ext_a[0:HALO, :] = ext_a[ts:ts + HALO, :]

    ext_b[HALO:HALO + ts, :] = p_u
    lane = lax.broadcasted_iota(jnp.int32, (1, MIX_W), 1)
    grp = jnp.right_shift(lane, 6)
    win = p_u
    acc = p_u
    sh = 1
    for gi, w in enumerate(POOL_WINDOWS):
        while sh < w:
            acc = acc + ext_b[HALO - sh:HALO - sh + ts, :]
            sh += 1
        if gi > 0:
            win = jnp.where(grp >= gi, acc, win)
        else:
            win = acc
    wlane = jnp.where(grp == 0, 2.0, jnp.where(grp == 1, 4.0, jnp.where(grp == 2, 8.0, 16.0)))
    tpos = (s * ts + lax.broadcasted_iota(jnp.int32, (ts, MIX_W), 0) + 1).astype(F32)
    cnt = jnp.minimum(tpos, wlane)
    pooled = win / cnt - p_u
    yb = (_dot(pooled.astype(BF16), wpool_ref[...]) + vrow(_R_POOL_B)) * vrow(_R_POOL_SCALE)
    ext_b[0:HALO, :] = ext_b[ts:ts + HALO, :]

    put(su_ref, s_u)
    u_cm = to_chunk_major(su_ref)
    bu = _dot(u_cm.astype(BF16), bbig_ref[...])
    bre = bu[:, :S5_LANES]
    bim = bu[:, S5_LANES:]
    c_re = coef_ref[0:1, :]
    c_im = coef_ref[1:2, :]
    st_ref[:, :S5_LANES] = c_re * bre - c_im * bim
    st_ref[:, S5_LANES:] = c_re * bim + c_im * bre
    a_re = are_ref[...]
    a_im = aim_ref[...]
    hr = st_ref[prow(0), :S5_LANES]
    hi = st_ref[prow(0), S5_LANES:]
    for pos in range(1, n_pos):
        hr, hi = (a_re * hr - a_im * hi + st_ref[prow(pos), :S5_LANES],
                  a_re * hi + a_im * hr + st_ref[prow(pos), S5_LANES:])
        st_ref[prow(pos), :S5_LANES] = hr
        st_ref[prow(pos), S5_LANES:] = hi
    first = row8 == 0
    fr = jnp.where(first, s5c_ref[0:1, :S5_LANES], pltpu.roll(hr, 1, axis=0))
    fi = jnp.where(first, s5c_ref[0:1, S5_LANES:], pltpu.roll(hi, 1, axis=0))
    for jj, shift in enumerate((1, 2, 4)):
        wr = wre_ref[prow(jj), :]
        wi = wim_ref[prow(jj), :]
        sr = pltpu.roll(fr, shift, axis=0)
        si = pltpu.roll(fi, shift, axis=0)
        fr, fi = fr + (wr * sr - wi * si), fi + (wr * si + wi * sr)
    al_re = alre_ref[...]
    al_im = alim_ref[...]
    nxt_re = al_re * fr - al_im * fi + hr
    nxt_im = al_re * fi + al_im * fr + hi
    s5c_ref[0:1, :S5_LANES] = nxt_re[SUBLANES - 1:SUBLANES, :]
    s5c_ref[0:1, S5_LANES:] = nxt_im[SUBLANES - 1:SUBLANES, :]
    for pos in range(n_pos):
        pr = ptre_ref[prow(pos), :]
        pi = ptim_ref[prow(pos), :]
        st_ref[prow(pos), :S5_LANES] = st_ref[prow(pos), :S5_LANES] + (pr * fr - pi * fi)
        st_ref[prow(pos), S5_LANES:] = st_ref[prow(pos), S5_LANES:] + (pr * fi + pi * fr)
    yc = (_dot(st_ref[:, :S5_LANES].astype(BF16), cbig_ref[:S5_LANES, :])
          + _dot(st_ref[:, S5_LANES:].astype(BF16), cbig_ref[S5_LANES:, :]))
    yc = yc + vrow(_R_S5_D) * u_cm
    yc = jax.nn.gelu(yc)
    yc = yc * jax.nn.sigmoid(_dot(yc.astype(BF16), glu_ref[...]) + vrow(_R_GLU_B))
    for pos in range(n_pos):
        store_time_major(yc_ref, pos, yc[prow(pos), :])
    yc = get(yc_ref)

    ext_d[HALO:HALO + ts, :] = l_x
    xc = ext_d[HALO - 3:HALO - 3 + ts, :] * vrow(_R_CD0)
    xc = xc + ext_d[HALO - 2:HALO - 2 + ts, :] * vrow(_R_CD1)
    xc = xc + ext_d[HALO - 1:HALO - 1 + ts, :] * vrow(_R_CD2)
    xc = xc + ext_d[HALO:HALO + ts, :] * vrow(_R_CD3)
    xc = xc + vrow(_R_CONV_B)
    ext_d[0:HALO, :] = ext_d[ts:ts + HALO, :]
    put(xc_ref, xc)
    xc = to_chunk_major(xc_ref)
    xcb = xc.astype(BF16)
    r_gate = jax.nn.sigmoid(_dot(xcb, wa_ref[...]) + vrow(_R_BA))
    i_gate = jax.nn.sigmoid(_dot(xcb, wx_ref[...]) + vrow(_R_BX))
    z = -vrow(_R_LAM)
    softplus = jnp.maximum(z, 0.0) + jnp.log1p(jnp.exp(-jnp.abs(z)))
    log_a = -LRU_C * r_gate * softplus
    a_t = jnp.exp(log_a)
    mult = jnp.sqrt(1.0 - a_t * a_t)
    b_t = mult * (i_gate * xc)
    hh = b_t[prow(0), :]
    aa = a_t[prow(0), :]
    h_loc = [hh]
    a_cum = [aa]
    for pos in range(1, n_pos):
        a_pos = a_t[prow(pos), :]
        hh = a_pos * hh + b_t[prow(pos), :]
        aa = a_pos * aa
        h_loc.append(hh)
        a_cum.append(aa)
    row8w = lax.broadcasted_iota(jnp.int32, (SUBLANES, MIX_W), 0)
    f = jnp.where(row8w == 0, lruc_ref[0:1, :], pltpu.roll(hh, 1, axis=0))
    m = pltpu.roll(aa, 1, axis=0)
    for shift in (1, 2, 4):
        keep = row8w >= shift
        f_s = pltpu.roll(f, shift, axis=0)
        m_s = pltpu.roll(m, shift, axis=0)
        f = jnp.where(keep, m * f_s + f, f)
        m = jnp.where(keep, m * m_s, m)
    lruc_ref[0:1, :] = (aa * f + hh)[SUBLANES - 1:SUBLANES, :]
    for pos in range(n_pos):
        store_time_major(lb_ref, pos, h_loc[pos] + a_cum[pos] * f)
    yd = get(lb_ref) * jax.nn.gelu(l_g)

    merged = None
    for k, yk in enumerate((ya, yb, yc, yd)):
        gate = jax.nn.sigmoid(_dot(xnb, wg_ref[:, k * D_MODEL:(k + 1) * D_MODEL])
                              + bg_ref[:, k * D_MODEL:(k + 1) * D_MODEL])
        term = gate * _dot(yk.astype(BF16), bw_ref[k])
        merged = term if merged is None else merged + term
    o_ref[...] = h + _dot(merged.astype(BF16), wout_ref[...])


def _const_spec(shape):
    nd = len(shape)
    return pl.BlockSpec(shape, lambda b, s: (0,) * nd)


def _mixer(h, p, batch, seq):
    ts = SEQ_TILE
    ns = seq // ts
    t = batch * seq
    weights = (p["g1"], p["w_in"], p["vec"], p["s5v"], p["wpool"], p["bbig"], p["cbig"], p["glu_w"],
               p["wa"], p["wx"], p["wg"], p["bg"], p["bw"], p["w_out"])
    in_specs = [pl.BlockSpec((ts, D_MODEL), lambda b, s: (b * ns + s, 0))]
    in_specs += [_const_spec(w.shape) for w in weights]
    return pl.pallas_call(
        _mixer_kernel,
        grid=(batch, ns),
        in_specs=in_specs,
        out_specs=pl.BlockSpec((ts, D_MODEL), lambda b, s: (b * ns + s, 0)),
        out_shape=jax.ShapeDtypeStruct((t, D_MODEL), F32),
        scratch_shapes=[
            pltpu.VMEM((HALO + ts, MIX_W), F32),
            pltpu.VMEM((HALO + ts, MIX_W), F32),
            pltpu.VMEM((HALO + ts, MIX_W), F32),
            pltpu.VMEM((ts, 2 * S5_LANES), F32),
            pltpu.VMEM((2, ts, LANES), F32),
            pltpu.VMEM((2, ts, LANES), F32),
            pltpu.VMEM((2, ts, LANES), F32),
            pltpu.VMEM((2, ts, LANES), F32),
            pltpu.VMEM((SUBLANES, 2 * S5_LANES), F32),
            pltpu.VMEM((SUBLANES, MIX_W), F32),
            pltpu.VMEM((SUBLANES, S5_LANES), F32),
            pltpu.VMEM((SUBLANES, S5_LANES), F32),
            pltpu.VMEM((SUBLANES, S5_LANES), F32),
            pltpu.VMEM((SUBLANES, S5_LANES), F32),
            pltpu.VMEM((3 * SUBLANES, S5_LANES), F32),
            pltpu.VMEM((3 * SUBLANES, S5_LANES), F32),
            pltpu.VMEM((ts, S5_LANES), F32),
            pltpu.VMEM((ts, S5_LANES), F32),
            pltpu.VMEM((SUBLANES, S5_LANES), F32),
        ],
        compiler_params=pltpu.CompilerParams(
            dimension_semantics=("arbitrary", "arbitrary"),
            vmem_limit_bytes=VMEM_LIMIT),
        name="mixer",
    )(h, *weights)


def _router_kernel(h_ref, g2_ref, wr_ref, br_ref, xn_ref, ri_ref, wcol_ref, cnt_ref, carry_ref):
    tm = h_ref.shape[0]
    i = pl.program_id(0)

    @pl.when(i == 0)
    def _():
        carry_ref[...] = jnp.zeros(carry_ref.shape, F32)

    xn = _rms(h_ref[...], g2_ref[...])
    xn_ref[...] = _pack_rows(xn)
    logits = lax.dot_general(wr_ref[...], xn.astype(BF16), (((1,), (1,)), ((), ())),
                             preferred_element_type=F32) + br_ref[...]
    row8 = lax.broadcasted_iota(jnp.int32, (SUBLANES, tm), 0)
    neg_inf = jnp.float32(-jnp.inf)
    gl = jnp.where(row8 < N_GROUPS, logits[0:SUBLANES, :], neg_inf)
    gmax = jnp.max(gl, axis=0, keepdims=True)
    ge = jnp.exp(gl - gmax)
    gp = ge / jnp.sum(ge, axis=0, keepdims=True)
    g_val = jnp.max(gp, axis=0, keepdims=True)
    g_idx = jnp.min(jnp.where(gp == g_val, row8, SUBLANES), axis=0, keepdims=True)
    sel = logits[4 * SUBLANES:5 * SUBLANES, :]
    for g in (2, 1, 0):
        sel = jnp.where(g_idx == g, logits[(g + 1) * SUBLANES:(g + 2) * SUBLANES, :], sel)
    v1 = jnp.max(sel, axis=0, keepdims=True)
    i1 = jnp.min(jnp.where(sel == v1, row8, SUBLANES), axis=0, keepdims=True)
    sel2 = jnp.where(row8 == i1, neg_inf, sel)
    v2 = jnp.max(sel2, axis=0, keepdims=True)
    i2 = jnp.min(jnp.where(sel2 == v2, row8, SUBLANES), axis=0, keepdims=True)
    e2 = jnp.exp(v2 - v1)
    denom = 1.0 + e2
    w1 = (1.0 / denom) * g_val
    w2 = (e2 / denom) * g_val
    eid0 = g_idx * EXP_PER_GROUP + i1
    eid1 = g_idx * EXP_PER_GROUP + i2
    e32 = lax.broadcasted_iota(jnp.int32, (N_EXPERTS, tm), 0)
    oh0 = (e32 == eid0).astype(F32)
    oh1 = (e32 == eid1).astype(F32)
    oh = oh0 + oh1
    before = (lax.broadcasted_iota(jnp.int32, (tm, tm), 0)
              < lax.broadcasted_iota(jnp.int32, (tm, tm), 1)).astype(BF16)
    base = _dot(oh.astype(BF16), before) + carry_ref[:, 0:1]
    rank0 = jnp.sum(oh0 * base, axis=0, keepdims=True)
    rank1 = jnp.sum(oh1 * base, axis=0, keepdims=True)
    new_carry = carry_ref[...] + jnp.sum(oh, axis=1, keepdims=True)
    carry_ref[...] = new_carry
    cnt_ref[...] = new_carry.astype(jnp.int32)
    ri_ref[...] = jnp.zeros(ri_ref.shape, jnp.int32)
    ri_ref[0:1, :] = eid0
    ri_ref[1:2, :] = eid1
    ri_ref[2:3, :] = rank0.astype(jnp.int32)
    ri_ref[3:4, :] = rank1.astype(jnp.int32)
    rows = lax.broadcasted_iota(jnp.int32, (ROUTER_ROWS, tm), 0)
    wrows = jnp.where(rows == 0, w1, jnp.where(rows == 1, w2, 0.0))
    wcol_ref[...] = wrows.T


def _router(h, g2, wr_t, br_col):
    t = h.shape[0]
    tm = ROUTE_TILE
    return pl.pallas_call(
        _router_kernel,
        grid=(t // tm,),
        in_specs=[
            pl.BlockSpec((tm, D_MODEL), lambda i: (i, 0)),
            pl.BlockSpec((1, D_MODEL), lambda i: (0, 0)),
            pl.BlockSpec((ROUTER_ROWS, D_MODEL), lambda i: (0, 0)),
            pl.BlockSpec((ROUTER_ROWS, 1), lambda i: (0, 0)),
        ],
        out_specs=[
            pl.BlockSpec((tm, ROW_WORDS), lambda i: (i, 0)),
            pl.BlockSpec((SUBLANES, tm), lambda i: (0, i)),
            pl.BlockSpec((tm, ROUTER_ROWS), lambda i: (i, 0)),
            pl.BlockSpec((N_EXPERTS, 128), lambda i: (0, 0)),
        ],
        out_shape=[
            jax.ShapeDtypeStruct((t, ROW_WORDS), jnp.uint32),
            jax.ShapeDtypeStruct((SUBLANES, t), jnp.int32),
            jax.ShapeDtypeStruct((t, ROUTER_ROWS), F32),
            jax.ShapeDtypeStruct((N_EXPERTS, 128), jnp.int32),
        ],
        scratch_shapes=[pltpu.VMEM((N_EXPERTS, 128), F32)],
        compiler_params=pltpu.CompilerParams(
            dimension_semantics=("arbitrary",), vmem_limit_bytes=VMEM_LIMIT),
        name="router",
    )(h, g2, wr_t, br_col)


def _dest_kernel(ri_ref, ps_ref, o_ref):
    tm = ri_ref.shape[1]
    e32 = lax.broadcasted_iota(jnp.int32, (N_EXPERTS, tm), 0)
    o_ref[...] = jnp.zeros(o_ref.shape, jnp.int32)
    for k in range(2):
        start = jnp.sum(jnp.where(e32 == ri_ref[k:k + 1, :], ps_ref[...], 0.0), axis=0, keepdims=True)
        o_ref[k:k + 1, :] = start.astype(jnp.int32) + ri_ref[2 + k:3 + k, :]


def _dest(route_i, pad_start_col):
    t = route_i.shape[1]
    tm = DEST_TILE
    return pl.pallas_call(
        _dest_kernel,
        grid=(t // tm,),
        in_specs=[
            pl.BlockSpec((SUBLANES, tm), lambda i: (0, i)),
            pl.BlockSpec((N_EXPERTS, 1), lambda i: (0, 0)),
        ],
        out_specs=pl.BlockSpec((SUBLANES, tm), lambda i: (0, i)),
        out_shape=jax.ShapeDtypeStruct((SUBLANES, t), jnp.int32),
        compiler_params=pltpu.CompilerParams(dimension_semantics=("arbitrary",)),
        name="dest",
    )(route_i, pad_start_col)


def _dispatch_kernel(dest_ref, fs_ref, fl_ref, meta_ref, x_ref, xs_ref, zbuf, sem, zsem):
    tm = x_ref.shape[0]
    bm = zbuf.shape[0]
    n_blocks = xs_ref.shape[0] // bm
    i = pl.program_id(0)
    n_tok = pl.num_programs(0) * tm
    base = i * tm

    def pad_copy(off, p):
        return pltpu.make_async_copy(zbuf.at[pl.ds(0, p), :], xs_ref.at[pl.ds(off, p), :], zsem)

    def for_each_fill(act):
        def segment(e, _):
            off = fs_ref[e]
            n = fl_ref[e]
            head = n & (SUBLANES - 1)
            for j in range(SUBLANES - 1):
                @pl.when(j < head)
                def _(off=off, j=j):
                    act(pad_copy(off + j, 1))

            off = pl.multiple_of(off + head, SUBLANES)
            for p in PAD_PIECES:
                if p < SUBLANES:
                    continue
                piece = n & p

                @pl.when(piece != 0)
                def _(off=off, p=p):
                    act(pad_copy(off, p))

                off = pl.multiple_of(off + piece, SUBLANES)
            return 0

        lax.fori_loop(0, N_EXPERTS, segment, 0)

        def tail(b, _):
            act(pad_copy(pl.multiple_of(b * bm, bm), bm))
            return 0

        lax.fori_loop(meta_ref[0], n_blocks, tail, 0)

    @pl.when(i == 0)
    def _():
        zbuf[...] = jnp.zeros(zbuf.shape, jnp.uint32)
        for_each_fill(lambda c: c.start())
        for_each_fill(lambda c: c.wait())

    def row_copy(r, d):
        return pltpu.make_async_copy(x_ref.at[pl.ds(r, 1), :], xs_ref.at[pl.ds(d, 1), :], sem)

    def issue(g, _):
        r0 = pl.multiple_of(g * DMA_UNROLL, DMA_UNROLL)
        for u in range(DMA_UNROLL):
            r = r0 + u
            row_copy(r, dest_ref[base + r]).start(priority=0)
            row_copy(r, dest_ref[n_tok + base + r]).start(priority=1)
        return 0

    lax.fori_loop(0, tm // DMA_UNROLL, issue, 0)

    for _ in range(2):
        pltpu.make_async_copy(x_ref, xs_ref.at[pl.ds(0, tm), :], sem).wait()


def _dispatch(dest_flat, fill_start, fill_len, meta, xn, n_rows):
    t = xn.shape[0]
    tm = DISPATCH_TILE
    return pl.pallas_call(
        _dispatch_kernel,
        grid_spec=pltpu.PrefetchScalarGridSpec(
            num_scalar_prefetch=4,
            grid=(t // tm,),
            in_specs=[pl.BlockSpec((tm, ROW_WORDS), lambda i, *_: (i, 0))],
            out_specs=pl.BlockSpec(memory_space=pl.ANY),
            scratch_shapes=[
                pltpu.VMEM((EXPERT_BLOCK, ROW_WORDS), jnp.uint32),
                pltpu.SemaphoreType.DMA(()),
                pltpu.SemaphoreType.DMA(()),
            ],
        ),
        out_shape=jax.ShapeDtypeStruct((n_rows, ROW_WORDS), jnp.uint32),
        compiler_params=pltpu.CompilerParams(dimension_semantics=("arbitrary",)),
        name="dispatch",
    )(dest_flat, fill_start, fill_len, meta, xn)


def _expert_kernel(ss_ref, nc_ref, meta_ref, xs_ref, w1_ref, w3_ref, w2_ref, y_ref,
                   xbuf, ybuf, w1b, w3b, w2b, insem, outsem):
    e = pl.program_id(0)
    ch = xbuf.shape[1]
    n_blocks = y_ref.shape[0] // ch
    start = ss_ref[e]
    nchunk = nc_ref[e]

    def rows(c):
        return pl.ds(pl.multiple_of(start + c * ch, ch), ch)

    def in_copy(c, slot):
        return pltpu.make_async_copy(xs_ref.at[rows(c), :], xbuf.at[slot], insem.at[slot])

    def out_copy(c, slot):
        return pltpu.make_async_copy(ybuf.at[slot], y_ref.at[rows(c), :], outsem.at[slot])

    @pl.when(nchunk > 0)
    def _():
        in_copy(0, 0).start()

    w1b[...] = w1_ref[0, 0].astype(BF16)
    w3b[...] = w3_ref[0, 0].astype(BF16)
    w2b[...] = w2_ref[0, 0].astype(BF16)

    def body(c, _):
        slot = c % 2

        @pl.when(c + 1 < nchunk)
        def _():
            in_copy(c + 1, 1 - slot).start()

        in_copy(c, slot).wait()

        @pl.when(c >= 2)
        def _():
            out_copy(c - 2, slot).wait()

        lo, hi = _unpack_rows(xbuf[slot])
        xb = jnp.concatenate([lo.astype(BF16), hi.astype(BF16)], axis=1)
        hid = jax.nn.silu(_dot(xb, w1b[...])) * _dot(xb, w3b[...])
        ybuf[slot] = _pack_rows(_dot(hid.astype(BF16), w2b[...]))
        out_copy(c, slot).start()
        return 0

    lax.fori_loop(0, nchunk, body, 0)

    @pl.when(nchunk >= 2)
    def _():
        out_copy(nchunk - 2, nchunk % 2).wait()

    @pl.when(nchunk >= 1)
    def _():
        out_copy(nchunk - 1, (nchunk - 1) % 2).wait()

    @pl.when(e == pl.num_programs(0) - 1)
    def _():
        ybuf[0] = jnp.zeros(ybuf.shape[1:], jnp.uint32)

        def tail_copy(b):
            return pltpu.make_async_copy(
                ybuf.at[0], y_ref.at[pl.ds(pl.multiple_of(b * ch, ch), ch), :], outsem.at[0])

        def tail_start(b, _):
            tail_copy(b).start()
            return 0

        def tail_wait(b, _):
            tail_copy(b).wait()
            return 0

        lax.fori_loop(meta_ref[0], n_blocks, tail_start, 0)
        lax.fori_loop(meta_ref[0], n_blocks, tail_wait, 0)


def _experts(layer, seg_start, seg_chunks, meta, xs, w1, w3, w2):
    n_rows = xs.shape[0]
    ch = EXPERT_BLOCK

    def w_map(e, *_):
        return (layer, e, 0, 0)

    return pl.pallas_call(
        _expert_kernel,
        grid_spec=pltpu.PrefetchScalarGridSpec(
            num_scalar_prefetch=3,
            grid=(N_EXPERTS,),
            in_specs=[
                pl.BlockSpec(memory_space=pl.ANY),
                pl.BlockSpec((1, 1, D_MODEL, D_EXPERT), w_map),
                pl.BlockSpec((1, 1, D_MODEL, D_EXPERT), w_map),
                pl.BlockSpec((1, 1, D_EXPERT, D_MODEL), w_map),
            ],
            out_specs=pl.BlockSpec(memory_space=pl.ANY),
            scratch_shapes=[
                pltpu.VMEM((2, ch, ROW_WORDS), jnp.uint32),
                pltpu.VMEM((2, ch, ROW_WORDS), jnp.uint32),
                pltpu.VMEM((D_MODEL, D_EXPERT), BF16),
                pltpu.VMEM((D_MODEL, D_EXPERT), BF16),
                pltpu.VMEM((D_EXPERT, D_MODEL), BF16),
                pltpu.SemaphoreType.DMA((2,)),
                pltpu.SemaphoreType.DMA((2,)),
            ],
        ),
        out_shape=jax.ShapeDtypeStruct((n_rows, ROW_WORDS), jnp.uint32),
        compiler_params=pltpu.CompilerParams(
            dimension_semantics=("arbitrary",), vmem_limit_bytes=VMEM_LIMIT),
        name="experts",
    )(seg_start, seg_chunks, meta, xs, w1, w3, w2)


def _combine_kernel(dest_ref, h_ref, wcol_ref, g_ref, y_ref, o_ref, buf, sem, *, final_norm):
    tc = h_ref.shape[0]
    i = pl.program_id(0)
    n = pl.num_programs(0)

    def row_copy(p, slot, k, r):
        return pltpu.make_async_copy(y_ref.at[pl.ds(p, 1), :],
                                     buf.at[2 * slot + k, pl.ds(r, 1), :], sem.at[slot])

    n_tok = n * tc

    def issue(tile, slot):
        def body(g, _):
            r0 = pl.multiple_of(g * DMA_UNROLL, DMA_UNROLL)
            for u in range(DMA_UNROLL):
                r = r0 + u
                tok = tile * tc + r
                row_copy(dest_ref[tok], slot, 0, r).start(priority=0)
                row_copy(dest_ref[n_tok + tok], slot, 1, r).start(priority=1)
            return 0
        lax.fori_loop(0, tc // DMA_UNROLL, body, 0)

    @pl.when(i == 0)
    def _():
        issue(0, 0)

    @pl.when(i + 1 < n)
    def _():
        issue(i + 1, (i + 1) % 2)

    slot = i % 2

    for k in range(2):
        pltpu.make_async_copy(y_ref.at[pl.ds(0, tc), :], buf.at[2 * slot + k], sem.at[slot]).wait()
    w = wcol_ref[...]
    lo0, hi0 = _unpack_rows(buf[2 * slot])
    lo1, hi1 = _unpack_rows(buf[2 * slot + 1])
    moe = jnp.concatenate([w[:, 0:1] * lo0 + w[:, 1:2] * lo1,
                           w[:, 0:1] * hi0 + w[:, 1:2] * hi1], axis=1)
    out = h_ref[...] + moe
    if final_norm:
        out = _rms(out, g_ref[...])
    o_ref[...] = out


def _combine(dest_flat, h, wcol, g, y, final_norm):
    t = h.shape[0]
    tc = COMBINE_TILE
    return pl.pallas_call(
        functools.partial(_combine_kernel, final_norm=final_norm),
        grid_spec=pltpu.PrefetchScalarGridSpec(
            num_scalar_prefetch=1,
            grid=(t // tc,),
            in_specs=[
                pl.BlockSpec((tc, D_MODEL), lambda i, d: (i, 0)),
                pl.BlockSpec((tc, ROUTER_ROWS), lambda i, d: (i, 0)),
                pl.BlockSpec((1, D_MODEL), lambda i, d: (0, 0)),
                pl.BlockSpec(memory_space=pl.ANY),
            ],
            out_specs=pl.BlockSpec((tc, D_MODEL), lambda i, d: (i, 0)),
            scratch_shapes=[
                pltpu.VMEM((4, tc, ROW_WORDS), jnp.uint32),
                pltpu.SemaphoreType.DMA((2,)),
            ],
        ),
        out_shape=jax.ShapeDtypeStruct((t, D_MODEL), F32),
        compiler_params=pltpu.CompilerParams(
            dimension_semantics=("arbitrary",), vmem_limit_bytes=VMEM_LIMIT),
        name="combine",
    )(dest_flat, h, wcol, g, y)


def _block_diag(w):
    g, i, o = w.shape
    eye = jnp.eye(g, dtype=w.dtype)
    return jnp.einsum("gio,gk->giko", w, eye).reshape(g * i, g * o)


def _layer_params(l, a):
    row = lambda v: v.reshape(1, -1)
    vec_rows = [a["pool_b"][l].reshape(-1), a["pool_scale"][l], a["s5_d"][l], a["s5_glu_b"][l],
                a["lru_conv_b"][l], a["lru_ba"][l].reshape(-1), a["lru_bx"][l].reshape(-1),
                a["lru_lambda"][l],
                a["conv_a_w"][l][0], a["conv_a_w"][l][1], a["conv_a_w"][l][2],
                a["lru_conv_w"][l][0], a["lru_conv_w"][l][1], a["lru_conv_w"][l][2],
                a["lru_conv_w"][l][3], jnp.zeros((MIX_W,), F32)]
    s5v = jnp.stack([a["s5_lambda_re"][l].reshape(-1), a["s5_lambda_im"][l].reshape(-1),
                     jnp.repeat(a["s5_log_step"][l], S5_STATE)]
                    + [jnp.zeros((S5_LANES,), F32)] * 5)
    b_re = _block_diag(jnp.swapaxes(a["s5_b_re"][l], 1, 2))
    b_im = _block_diag(jnp.swapaxes(a["s5_b_im"][l], 1, 2))
    c_re = _block_diag(jnp.swapaxes(a["s5_c_re"][l], 1, 2))
    c_im = _block_diag(jnp.swapaxes(a["s5_c_im"][l], 1, 2))
    wr_t = jnp.zeros((ROUTER_ROWS, D_MODEL), F32)
    wr_t = wr_t.at[0:N_GROUPS].set(a["router_group_w"][l].T)
    wr_t = wr_t.at[SUBLANES:SUBLANES + N_EXPERTS].set(a["router_expert_w"][l].T)
    br = jnp.zeros((ROUTER_ROWS,), F32)
    br = br.at[0:N_GROUPS].set(a["router_group_b"][l])
    br = br.at[SUBLANES:SUBLANES + N_EXPERTS].set(a["router_expert_b"][l])
    return {
        "g1": row(a["norm1_g"][l]),
        "w_in": a["w_in"][l].astype(BF16),
        "vec": jnp.stack(vec_rows),
        "s5v": s5v,
        "wpool": _block_diag(a["pool_w"][l]).astype(BF16),
        "bbig": jnp.concatenate([b_re, b_im], axis=1).astype(BF16),
        "cbig": jnp.concatenate([c_re, -c_im], axis=0).astype(BF16),
        "glu_w": a["s5_glu_w"][l].astype(BF16),
        "wa": _block_diag(a["lru_wa"][l]).astype(BF16),
        "wx": _block_diag(a["lru_wx"][l]).astype(BF16),
        "wg": a["merge_gate_w"][l].astype(BF16),
        "bg": row(a["merge_gate_b"][l]),
        "bw": a["branch_w"][l].astype(BF16),
        "w_out": a["w_out"][l].astype(BF16),
        "g2": row(a["norm2_g"][l]),
        "wr_t": wr_t.astype(BF16),
        "br": br.reshape(ROUTER_ROWS, 1),
    }


def _moe(layer, h, p, w1, w3, w2, g_final, final_norm):
    t = h.shape[0]
    bm = EXPERT_BLOCK
    xn, route_i, wcol, counts = _router(h, p["g2"], p["wr_t"], p["br"])
    cnt = counts[:, 0]
    padded = ((cnt + bm - 1) // bm) * bm
    pad_end = jnp.cumsum(padded)
    pad_start = pad_end - padded
    n_rows = (-(-(2 * t) // bm)) * bm + N_EXPERTS * bm
    n_blocks = n_rows // bm
    meta = (pad_end[-1:] // bm).astype(jnp.int32)
    dest = _dest(route_i, pad_start.astype(F32).reshape(N_EXPERTS, 1))
    dest_flat = dest[0:2].reshape(-1)
    xs = _dispatch(dest_flat, (pad_start + cnt).astype(jnp.int32), (padded - cnt).astype(jnp.int32),
                   meta, xn, n_rows)
    y = _experts(layer, pad_start.astype(jnp.int32), (padded // bm).astype(jnp.int32), meta,
                 xs, w1, w3, w2)
    return _combine(dest_flat, h, wcol, g_final, y, final_norm)


def kernel(x, norm1_g, w_in, conv_a_w, pool_w, pool_b, pool_scale, s5_lambda_re, s5_lambda_im, s5_log_step, s5_b_re, s5_b_im, s5_c_re, s5_c_im, s5_d, s5_glu_w, s5_glu_b, lru_conv_w, lru_conv_b, lru_wa, lru_ba, lru_wx, lru_bx, lru_lambda, merge_gate_w, merge_gate_b, branch_w, w_out, norm2_g, router_group_w, router_group_b, router_expert_w, router_expert_b, expert_w1, expert_w3, expert_w2, final_norm_g):
    a = dict(norm1_g=norm1_g, w_in=w_in, conv_a_w=conv_a_w, pool_w=pool_w, pool_b=pool_b,
             pool_scale=pool_scale, s5_lambda_re=s5_lambda_re, s5_lambda_im=s5_lambda_im,
             s5_log_step=s5_log_step, s5_b_re=s5_b_re, s5_b_im=s5_b_im, s5_c_re=s5_c_re,
             s5_c_im=s5_c_im, s5_d=s5_d, s5_glu_w=s5_glu_w, s5_glu_b=s5_glu_b,
             lru_conv_w=lru_conv_w, lru_conv_b=lru_conv_b, lru_wa=lru_wa, lru_ba=lru_ba,
             lru_wx=lru_wx, lru_bx=lru_bx, lru_lambda=lru_lambda, merge_gate_w=merge_gate_w,
             merge_gate_b=merge_gate_b, branch_w=branch_w, w_out=w_out, norm2_g=norm2_g,
             router_group_w=router_group_w, router_group_b=router_group_b,
             router_expert_w=router_expert_w, router_expert_b=router_expert_b)
    batch, seq, d = x.shape
    depth = norm1_g.shape[0]
    h = x.reshape(batch * seq, d)
    g_final = final_norm_g.reshape(1, d)
    for l in range(depth):
        p = _layer_params(l, a)
        h = _mixer(h, p, batch, seq)
        h = _moe(l, h, p, expert_w1, expert_w3, expert_w2, g_final, l == depth - 1)
    return h.reshape(batch, seq, d)
```

```python
import functools

import jax
import jax.numpy as jnp
from jax import lax
from jax.experimental import pallas as pl
from jax.experimental.pallas import tpu as pltpu

F32 = jnp.float32
BF16 = jnp.bfloat16

D_MODEL = 1024
MIX_W = 256
N_BRANCH = 4
POOL_WINDOWS = (2, 4, 8, 16)
POOL_GC = 64
S5_GROUPS = 16
S5_GROUP_CH = 16
S5_STATE = 64
S5_LANES = S5_GROUPS * S5_STATE
S5_EIG_CLIP = -1e-4
LRU_HEADS = 4
LRU_C = 8.0
IN_COLS = 1792
N_GROUPS = 4
EXP_PER_GROUP = 8
N_EXPERTS = 32
D_EXPERT = 512
EPS = 1e-6

SUBLANES = 8
LANES = 128
ROW_WORDS = D_MODEL // 2
HALO = 16
SEQ_TILE = 256
ROUTE_TILE = 512
DISPATCH_TILE = 512
COMBINE_TILE = 256
EXPERT_BLOCK = 256
ROUTER_ROWS = 128
DEST_TILE = 2048
DMA_UNROLL = 8
CHUNK_DMA_PRIORITY = 1
PAD_PIECES = tuple(EXPERT_BLOCK >> (k + 1) for k in range(EXPERT_BLOCK.bit_length() - 1))
VMEM_LIMIT = 56 * 1024 * 1024

(_R_POOL_B, _R_POOL_SCALE, _R_S5_D, _R_GLU_B, _R_CONV_B, _R_BA, _R_BX, _R_LAM,
 _R_CA0, _R_CA1, _R_CA2, _R_CD0, _R_CD1, _R_CD2, _R_CD3) = range(15)


def _rms(x, g):
    return x * lax.rsqrt(jnp.mean(x * x, axis=-1, keepdims=True) + EPS) * g


def _dot(a, b):
    return jnp.dot(a, b, preferred_element_type=F32)


def _pack_rows(x):
    half = x.shape[1] // 2
    lo = lax.bitcast_convert_type(x[:, :half].astype(BF16).astype(F32), jnp.uint32)
    hi = lax.bitcast_convert_type(x[:, half:].astype(BF16).astype(F32), jnp.uint32)
    return (hi & jnp.uint32(0xFFFF0000)) | (lo >> 16)


def _unpack_rows(w):
    lo = lax.bitcast_convert_type(w << 16, F32)
    hi = lax.bitcast_convert_type(w & jnp.uint32(0xFFFF0000), F32)
    return lo, hi


def _mixer_kernel(h_ref, g1_ref, win_ref, vec_ref, s5v_ref, wpool_ref, bbig_ref, cbig_ref,
                  glu_ref, wa_ref, wx_ref, wg_ref, bg_ref, bw_ref, wout_ref,
                  o_ref,
                  ext_a, ext_b, ext_d, st_ref, su_ref, xc_ref, yc_ref, lb_ref, s5c_ref, lruc_ref,
                  are_ref, aim_ref, alre_ref, alim_ref, wre_ref, wim_ref, ptre_ref, ptim_ref,
                  coef_ref):
    ts = h_ref.shape[0]
    n_pos = ts // SUBLANES
    s = pl.program_id(1)
    row8 = lax.broadcasted_iota(jnp.int32, (SUBLANES, S5_LANES), 0)

    def put(ref, val):
        for half in range(2):
            ref[half] = val[:, half * LANES:(half + 1) * LANES]

    def get(ref):
        return jnp.concatenate([ref[0], ref[1]], axis=1)

    def to_chunk_major(ref):
        return jnp.concatenate(
            [jnp.concatenate([ref[half, pl.ds(pos, SUBLANES, stride=n_pos), :] for pos in range(n_pos)],
                             axis=0) for half in range(2)], axis=1)

    def store_time_major(ref, pos, val):
        for half in range(2):
            ref[half, pl.ds(pos, SUBLANES, stride=n_pos), :] = val[:, half * LANES:(half + 1) * LANES]

    def prow(pos):
        return slice(pos * SUBLANES, (pos + 1) * SUBLANES)

    @pl.when(s == 0)
    def _start_of_sequence():
        zeros_halo = jnp.zeros((HALO, MIX_W), F32)
        ext_a[0:HALO, :] = zeros_halo
        ext_b[0:HALO, :] = zeros_halo
        ext_d[0:HALO, :] = zeros_halo
        s5c_ref[...] = jnp.zeros(s5c_ref.shape, F32)
        lruc_ref[...] = jnp.zeros(lruc_ref.shape, F32)

    @pl.when((s == 0) & (pl.program_id(0) == 0))
    def _s5_discretisation():
        lam_re = jnp.minimum(s5v_ref[0:1, :], S5_EIG_CLIP)
        lam_im = s5v_ref[1:2, :]
        dt = jnp.exp(s5v_ref[2:3, :])
        xr = lam_re * dt
        th = lam_im * dt

        def power(k):
            ek = jnp.exp(k * xr)
            return ek * jnp.cos(k * th), ek * jnp.sin(k * th)

        a_re, a_im = power(1.0)
        den = lam_re * lam_re + lam_im * lam_im
        coef_ref[0:1, :] = ((a_re - 1.0) * lam_re + a_im * lam_im) / den
        coef_ref[1:2, :] = (a_im * lam_re - (a_re - 1.0) * lam_im) / den
        are_ref[...] = jnp.broadcast_to(a_re, are_ref.shape)
        aim_ref[...] = jnp.broadcast_to(a_im, aim_ref.shape)
        al_re, al_im = power(float(n_pos))
        alre_ref[...] = jnp.broadcast_to(al_re, alre_ref.shape)
        alim_ref[...] = jnp.broadcast_to(al_im, alim_ref.shape)
        for j, sh in enumerate((1, 2, 4)):
            p_re, p_im = power(float(sh * n_pos))
            keep = row8 >= sh
            wre_ref[prow(j), :] = jnp.where(keep, p_re, 0.0)
            wim_ref[prow(j), :] = jnp.where(keep, p_im, 0.0)
        kk = (lax.broadcasted_iota(jnp.int32, (n_pos, S5_LANES), 0) + 1).astype(F32)
        t_re, t_im = power(kk)
        for pos in range(n_pos):
            ptre_ref[prow(pos), :] = jnp.broadcast_to(t_re[pos:pos + 1, :], (SUBLANES, S5_LANES))
            ptim_ref[prow(pos), :] = jnp.broadcast_to(t_im[pos:pos + 1, :], (SUBLANES, S5_LANES))

    def vrow(r):
        return vec_ref[r:r + 1, :]

    h = h_ref[...]
    xn = _rms(h, g1_ref[...])
    xnb = xn.astype(BF16)
    proj = _dot(xnb, win_ref[...])
    a_b = proj[:, 0:256]
    a_c = proj[:, 256:512]
    a_x = proj[:, 512:768]
    p_u = proj[:, 768:1024]
    s_u = proj[:, 1024:1280]
    l_x = proj[:, 1280:1536]
    l_g = proj[:, 1536:1792]

    ext_a[HALO:HALO + ts, :] = a_c * a_x
    conv = ext_a[HALO - 2:HALO - 2 + ts, :] * vrow(_R_CA0)
    conv = conv + ext_a[HALO - 1:HALO - 1 + ts, :] * vrow(_R_CA1)
    conv = conv + ext_a[HALO:HALO + ts, :] * vrow(_R_CA2)
    ya = a_b * conv
    ext_a[0:HALO, :] = ext_a[ts:ts + HALO, :]

    ext_b[HALO:HALO + ts, :] = p_u
    lane = lax.broadcasted_iota(jnp.int32, (1, MIX_W), 1)
    grp = jnp.right_shift(lane, 6)
    win = p_u
    acc = p_u
    sh = 1
    for gi, w in enumerate(POOL_WINDOWS):
        while sh < w:
            acc = acc + ext_b[HALO - sh:HALO - sh + ts, :]
            sh += 1
        if gi > 0:
            win = jnp.where(grp >= gi, acc, win)
        else:
            win = acc
    wlane = jnp.where(grp == 0, 2.0, jnp.where(grp == 1, 4.0, jnp.where(grp == 2, 8.0, 16.0)))
    tpos = (s * ts + lax.broadcasted_iota(jnp.int32, (ts, MIX_W), 0) + 1).astype(F32)
    cnt = jnp.minimum(tpos, wlane)
    pooled = win / cnt - p_u
    yb = (_dot(pooled.astype(BF16), wpool_ref[...]) + vrow(_R_POOL_B)) * vrow(_R_POOL_SCALE)
    ext_b[0:HALO, :] = ext_b[ts:ts + HALO, :]

    put(su_ref, s_u)
    u_cm = to_chunk_major(su_ref)
    bu = _dot(u_cm.astype(BF16), bbig_ref[...])
    bre = bu[:, :S5_LANES]
    bim = bu[:, S5_LANES:]
    c_re = coef_ref[0:1, :]
    c_im = coef_ref[1:2, :]
    st_ref[:, :S5_LANES] = c_re * bre - c_im * bim
    st_ref[:, S5_LANES:] = c_re * bim + c_im * bre
    a_re = are_ref[...]
    a_im = aim_ref[...]
    hr = st_ref[prow(0), :S5_LANES]
    hi = st_ref[prow(0), S5_LANES:]
    for pos in range(1, n_pos):
        hr, hi = (a_re * hr - a_im * hi + st_ref[prow(pos), :S5_LANES],
                  a_re * hi + a_im * hr + st_ref[prow(pos), S5_LANES:])
        st_ref[prow(pos), :S5_LANES] = hr
        st_ref[prow(pos), S5_LANES:] = hi
    first = row8 == 0
    fr = jnp.where(first, s5c_ref[0:1, :S5_LANES], pltpu.roll(hr, 1, axis=0))
    fi = jnp.where(first, s5c_ref[0:1, S5_LANES:], pltpu.roll(hi, 1, axis=0))
    for jj, shift in enumerate((1, 2, 4)):
        wr = wre_ref[prow(jj), :]
        wi = wim_ref[prow(jj), :]
        sr = pltpu.roll(fr, shift, axis=0)
        si = pltpu.roll(fi, shift, axis=0)
        fr, fi = fr + (wr * sr - wi * si), fi + (wr * si + wi * sr)
    al_re = alre_ref[...]
    al_im = alim_ref[...]
    nxt_re = al_re * fr - al_im * fi + hr
    nxt_im = al_re * fi + al_im * fr + hi
    s5c_ref[0:1, :S5_LANES] = nxt_re[SUBLANES - 1:SUBLANES, :]
    s5c_ref[0:1, S5_LANES:] = nxt_im[SUBLANES - 1:SUBLANES, :]
    for pos in range(n_pos):
        pr = ptre_ref[prow(pos), :]
        pi = ptim_ref[prow(pos), :]
        st_ref[prow(pos), :S5_LANES] = st_ref[prow(pos), :S5_LANES] + (pr * fr - pi * fi)
        st_ref[prow(pos), S5_LANES:] = st_ref[prow(pos), S5_LANES:] + (pr * fi + pi * fr)
    yc = (_dot(st_ref[:, :S5_LANES].astype(BF16), cbig_ref[:S5_LANES, :])
          + _dot(st_ref[:, S5_LANES:].astype(BF16), cbig_ref[S5_LANES:, :]))
    yc = yc + vrow(_R_S5_D) * u_cm
    yc = jax.nn.gelu(yc)
    yc = yc * jax.nn.sigmoid(_dot(yc.astype(BF16), glu_ref[...]) + vrow(_R_GLU_B))
    for pos in range(n_pos):
        store_time_major(yc_ref, pos, yc[prow(pos), :])
    yc = get(yc_ref)

    ext_d[HALO:HALO + ts, :] = l_x
    xc = ext_d[HALO - 3:HALO - 3 + ts, :] * vrow(_R_CD0)
    xc = xc + ext_d[HALO - 2:HALO - 2 + ts, :] * vrow(_R_CD1)
    xc = xc + ext_d[HALO - 1:HALO - 1 + ts, :] * vrow(_R_CD2)
    xc = xc + ext_d[HALO:HALO + ts, :] * vrow(_R_CD3)
    xc = xc + vrow(_R_CONV_B)
    ext_d[0:HALO, :] = ext_d[ts:ts + HALO, :]
    put(xc_ref, xc)
    xc = to_chunk_major(xc_ref)
    xcb = xc.astype(BF16)
    r_gate = jax.nn.sigmoid(_dot(xcb, wa_ref[...]) + vrow(_R_BA))
    i_gate = jax.nn.sigmoid(_dot(xcb, wx_ref[...]) + vrow(_R_BX))
    z = -vrow(_R_LAM)
    softplus = jnp.maximum(z, 0.0) + jnp.log1p(jnp.exp(-jnp.abs(z)))
    log_a = -LRU_C * r_gate * softplus
    a_t = jnp.exp(log_a)
    mult = jnp.sqrt(1.0 - a_t * a_t)
    b_t = mult * (i_gate * xc)
    hh = b_t[prow(0), :]
    aa = a_t[prow(0), :]
    h_loc = [hh]
    a_cum = [aa]
    for pos in range(1, n_pos):
        a_pos = a_t[prow(pos), :]
        hh = a_pos * hh + b_t[prow(pos), :]
        aa = a_pos * aa
        h_loc.append(hh)
        a_cum.append(aa)
    row8w = lax.broadcasted_iota(jnp.int32, (SUBLANES, MIX_W), 0)
    f = jnp.where(row8w == 0, lruc_ref[0:1, :], pltpu.roll(hh, 1, axis=0))
    m = pltpu.roll(aa, 1, axis=0)
    for shift in (1, 2, 4):
        keep = row8w >= shift
        f_s = pltpu.roll(f, shift, axis=0)
        m_s = pltpu.roll(m, shift, axis=0)
        f = jnp.where(keep, m * f_s + f, f)
        m = jnp.where(keep, m * m_s, m)
    lruc_ref[0:1, :] = (aa * f + hh)[SUBLANES - 1:SUBLANES, :]
    for pos in range(n_pos):
        store_time_major(lb_ref, pos, h_loc[pos] + a_cum[pos] * f)
    yd = get(lb_ref) * jax.nn.gelu(l_g)

    merged = None
    for k, yk in enumerate((ya, yb, yc, yd)):
        gate = jax.nn.sigmoid(_dot(xnb, wg_ref[:, k * D_MODEL:(k + 1) * D_MODEL])
                              + bg_ref[:, k * D_MODEL:(k + 1) * D_MODEL])
        term = gate * _dot(yk.astype(BF16), bw_ref[k])
        merged = term if merged is None else merged + term
    o_ref[...] = h + _dot(merged.astype(BF16), wout_ref[...])


def _const_spec(shape):
    nd = len(shape)
    return pl.BlockSpec(shape, lambda b, s: (0,) * nd, pipeline_mode=pl.Buffered(1))


def _mixer(h, p, batch, seq):
    ts = SEQ_TILE
    ns = seq // ts
    t = batch * seq
    weights = (p["g1"], p["w_in"], p["vec"], p["s5v"], p["wpool"], p["bbig"], p["cbig"], p["glu_w"],
               p["wa"], p["wx"], p["wg"], p["bg"], p["bw"], p["w_out"])
    in_specs = [pl.BlockSpec((ts, D_MODEL), lambda b, s: (b * ns + s, 0))]
    in_specs += [_const_spec(w.shape) for w in weights]
    return pl.pallas_call(
        _mixer_kernel,
        grid=(batch, ns),
        in_specs=in_specs,
        out_specs=pl.BlockSpec((ts, D_MODEL), lambda b, s: (b * ns + s, 0)),
        out_shape=jax.ShapeDtypeStruct((t, D_MODEL), F32),
        scratch_shapes=[
            pltpu.VMEM((HALO + ts, MIX_W), F32),
            pltpu.VMEM((HALO + ts, MIX_W), F32),
            pltpu.VMEM((HALO + ts, MIX_W), F32),
            pltpu.VMEM((ts, 2 * S5_LANES), F32),
            pltpu.VMEM((2, ts, LANES), F32),
            pltpu.VMEM((2, ts, LANES), F32),
            pltpu.VMEM((2, ts, LANES), F32),
            pltpu.VMEM((2, ts, LANES), F32),
            pltpu.VMEM((SUBLANES, 2 * S5_LANES), F32),
            pltpu.VMEM((SUBLANES, MIX_W), F32),
            pltpu.VMEM((SUBLANES, S5_LANES), F32),
            pltpu.VMEM((SUBLANES, S5_LANES), F32),
            pltpu.VMEM((SUBLANES, S5_LANES), F32),
            pltpu.VMEM((SUBLANES, S5_LANES), F32),
            pltpu.VMEM((3 * SUBLANES, S5_LANES), F32),
            pltpu.VMEM((3 * SUBLANES, S5_LANES), F32),
            pltpu.VMEM((ts, S5_LANES), F32),
            pltpu.VMEM((ts, S5_LANES), F32),
            pltpu.VMEM((SUBLANES, S5_LANES), F32),
        ],
        compiler_params=pltpu.CompilerParams(
            dimension_semantics=("arbitrary", "arbitrary"),
            vmem_limit_bytes=VMEM_LIMIT),
        name="mixer",
    )(h, *weights)


def _router_kernel(h_ref, g2_ref, wr_ref, br_ref, xn_ref, ri_ref, wcol_ref, cnt_ref, carry_ref):
    tm = h_ref.shape[0]
    i = pl.program_id(0)

    @pl.when(i == 0)
    def _():
        carry_ref[...] = jnp.zeros(carry_ref.shape, F32)

    xn = _rms(h_ref[...], g2_ref[...])
    xn_ref[...] = _pack_rows(xn)
    logits = lax.dot_general(wr_ref[...], xn.astype(BF16), (((1,), (1,)), ((), ())),
                             preferred_element_type=F32) + br_ref[...]
    row8 = lax.broadcasted_iota(jnp.int32, (SUBLANES, tm), 0)
    neg_inf = jnp.float32(-jnp.inf)
    gl = jnp.where(row8 < N_GROUPS, logits[0:SUBLANES, :], neg_inf)
    gmax = jnp.max(gl, axis=0, keepdims=True)
    ge = jnp.exp(gl - gmax)
    gp = ge / jnp.sum(ge, axis=0, keepdims=True)
    g_val = jnp.max(gp, axis=0, keepdims=True)
    g_idx = jnp.min(jnp.where(gp == g_val, row8, SUBLANES), axis=0, keepdims=True)
    sel = logits[4 * SUBLANES:5 * SUBLANES, :]
    for g in (2, 1, 0):
        sel = jnp.where(g_idx == g, logits[(g + 1) * SUBLANES:(g + 2) * SUBLANES, :], sel)
    v1 = jnp.max(sel, axis=0, keepdims=True)
    i1 = jnp.min(jnp.where(sel == v1, row8, SUBLANES), axis=0, keepdims=True)
    sel2 = jnp.where(row8 == i1, neg_inf, sel)
    v2 = jnp.max(sel2, axis=0, keepdims=True)
    i2 = jnp.min(jnp.where(sel2 == v2, row8, SUBLANES), axis=0, keepdims=True)
    e2 = jnp.exp(v2 - v1)
    denom = 1.0 + e2
    w1 = (1.0 / denom) * g_val
    w2 = (e2 / denom) * g_val
    eid0 = g_idx * EXP_PER_GROUP + i1
    eid1 = g_idx * EXP_PER_GROUP + i2
    e32 = lax.broadcasted_iota(jnp.int32, (N_EXPERTS, tm), 0)
    oh0 = (e32 == eid0).astype(F32)
    oh1 = (e32 == eid1).astype(F32)
    oh = oh0 + oh1
    before = (lax.broadcasted_iota(jnp.int32, (tm, tm), 0)
              < lax.broadcasted_iota(jnp.int32, (tm, tm), 1)).astype(BF16)
    base = _dot(oh.astype(BF16), before) + carry_ref[:, 0:1]
    rank0 = jnp.sum(oh0 * base, axis=0, keepdims=True)
    rank1 = jnp.sum(oh1 * base, axis=0, keepdims=True)
    new_carry = carry_ref[...] + jnp.sum(oh, axis=1, keepdims=True)
    carry_ref[...] = new_carry
    cnt_ref[...] = new_carry.astype(jnp.int32)
    ri_ref[...] = jnp.zeros(ri_ref.shape, jnp.int32)
    ri_ref[0:1, :] = eid0
    ri_ref[1:2, :] = eid1
    ri_ref[2:3, :] = rank0.astype(jnp.int32)
    ri_ref[3:4, :] = rank1.astype(jnp.int32)
    rows = lax.broadcasted_iota(jnp.int32, (ROUTER_ROWS, tm), 0)
    wrows = jnp.where(rows == 0, w1, jnp.where(rows == 1, w2, 0.0))
    wcol_ref[...] = wrows.T


def _router(h, g2, wr_t, br_col):
    t = h.shape[0]
    tm = ROUTE_TILE
    return pl.pallas_call(
        _router_kernel,
        grid=(t // tm,),
        in_specs=[
            pl.BlockSpec((tm, D_MODEL), lambda i: (i, 0)),
            pl.BlockSpec((1, D_MODEL), lambda i: (0, 0)),
            pl.BlockSpec((ROUTER_ROWS, D_MODEL), lambda i: (0, 0)),
            pl.BlockSpec((ROUTER_ROWS, 1), lambda i: (0, 0)),
        ],
        out_specs=[
            pl.BlockSpec((tm, ROW_WORDS), lambda i: (i, 0)),
            pl.BlockSpec((SUBLANES, tm), lambda i: (0, i)),
            pl.BlockSpec((tm, ROUTER_ROWS), lambda i: (i, 0)),
            pl.BlockSpec((N_EXPERTS, 128), lambda i: (0, 0)),
        ],
        out_shape=[
            jax.ShapeDtypeStruct((t, ROW_WORDS), jnp.uint32),
            jax.ShapeDtypeStruct((SUBLANES, t), jnp.int32),
            jax.ShapeDtypeStruct((t, ROUTER_ROWS), F32),
            jax.ShapeDtypeStruct((N_EXPERTS, 128), jnp.int32),
        ],
        scratch_shapes=[pltpu.VMEM((N_EXPERTS, 128), F32)],
        compiler_params=pltpu.CompilerParams(
            dimension_semantics=("arbitrary",), vmem_limit_bytes=VMEM_LIMIT),
        name="router",
    )(h, g2, wr_t, br_col)


def _dest_kernel(ri_ref, ps_ref, o_ref):
    tm = ri_ref.shape[1]
    e32 = lax.broadcasted_iota(jnp.int32, (N_EXPERTS, tm), 0)
    o_ref[...] = jnp.zeros(o_ref.shape, jnp.int32)
    for k in range(2):
        start = jnp.sum(jnp.where(e32 == ri_ref[k:k + 1, :], ps_ref[...], 0.0), axis=0, keepdims=True)
        o_ref[k:k + 1, :] = start.astype(jnp.int32) + ri_ref[2 + k:3 + k, :]


def _dest(route_i, pad_start_col):
    t = route_i.shape[1]
    tm = DEST_TILE
    return pl.pallas_call(
        _dest_kernel,
        grid=(t // tm,),
        in_specs=[
            pl.BlockSpec((SUBLANES, tm), lambda i: (0, i)),
            pl.BlockSpec((N_EXPERTS, 1), lambda i: (0, 0)),
        ],
        out_specs=pl.BlockSpec((SUBLANES, tm), lambda i: (0, i)),
        out_shape=jax.ShapeDtypeStruct((SUBLANES, t), jnp.int32),
        compiler_params=pltpu.CompilerParams(dimension_semantics=("arbitrary",)),
        name="dest",
    )(route_i, pad_start_col)


def _dispatch_kernel(dest_ref, fs_ref, fl_ref, meta_ref, x_ref, xs_ref, zbuf, sem, zsem):
    tm = x_ref.shape[0]
    bm = zbuf.shape[0]
    n_blocks = xs_ref.shape[0] // bm
    i = pl.program_id(0)
    n_tok = pl.num_programs(0) * tm
    base = i * tm

    def pad_copy(off, p):
        return pltpu.make_async_copy(zbuf.at[pl.ds(0, p), :], xs_ref.at[pl.ds(off, p), :], zsem)

    def for_each_fill(act):
        def segment(e, _):
            off = fs_ref[e]
            n = fl_ref[e]
            head = n & (SUBLANES - 1)
            for j in range(SUBLANES - 1):
                @pl.when(j < head)
                def _(off=off, j=j):
                    act(pad_copy(off + j, 1))

            off = pl.multiple_of(off + head, SUBLANES)
            for p in PAD_PIECES:
                if p < SUBLANES:
                    continue
                piece = n & p

                @pl.when(piece != 0)
                def _(off=off, p=p):
                    act(pad_copy(off, p))

                off = pl.multiple_of(off + piece, SUBLANES)
            return 0

        lax.fori_loop(0, N_EXPERTS, segment, 0)

        def tail(b, _):
            act(pad_copy(pl.multiple_of(b * bm, bm), bm))
            return 0

        lax.fori_loop(meta_ref[0], n_blocks, tail, 0)

    @pl.when(i == 0)
    def _():
        zbuf[...] = jnp.zeros(zbuf.shape, jnp.uint32)
        for_each_fill(lambda c: c.start())
        for_each_fill(lambda c: c.wait())

    def row_copy(r, d):
        return pltpu.make_async_copy(x_ref.at[pl.ds(r, 1), :], xs_ref.at[pl.ds(d, 1), :], sem)

    def issue(g, _):
        r0 = pl.multiple_of(g * DMA_UNROLL, DMA_UNROLL)
        for u in range(DMA_UNROLL):
            r = r0 + u
            row_copy(r, dest_ref[base + r]).start(priority=0)
            row_copy(r, dest_ref[n_tok + base + r]).start(priority=1)
        return 0

    lax.fori_loop(0, tm // DMA_UNROLL, issue, 0)

    for _ in range(2):
        pltpu.make_async_copy(x_ref, xs_ref.at[pl.ds(0, tm), :], sem).wait()


def _dispatch(dest_flat, fill_start, fill_len, meta, xn, n_rows):
    t = xn.shape[0]
    tm = DISPATCH_TILE
    return pl.pallas_call(
        _dispatch_kernel,
        grid_spec=pltpu.PrefetchScalarGridSpec(
            num_scalar_prefetch=4,
            grid=(t // tm,),
            in_specs=[pl.BlockSpec((tm, ROW_WORDS), lambda i, *_: (i, 0))],
            out_specs=pl.BlockSpec(memory_space=pl.ANY),
            scratch_shapes=[
                pltpu.VMEM((EXPERT_BLOCK, ROW_WORDS), jnp.uint32),
                pltpu.SemaphoreType.DMA(()),
                pltpu.SemaphoreType.DMA(()),
            ],
        ),
        out_shape=jax.ShapeDtypeStruct((n_rows, ROW_WORDS), jnp.uint32),
        compiler_params=pltpu.CompilerParams(dimension_semantics=("arbitrary",)),
        name="dispatch",
    )(dest_flat, fill_start, fill_len, meta, xn)


def _expert_kernel(ss_ref, nc_ref, meta_ref, xs_ref, w1_ref, w3_ref, w2_ref, y_ref,
                   xbuf, ybuf, w1b, w3b, w2b, insem, outsem):
    e = pl.program_id(0)
    ch = xbuf.shape[1]
    n_blocks = y_ref.shape[0] // ch
    start = ss_ref[e]
    nchunk = nc_ref[e]

    def rows(c):
        return pl.ds(pl.multiple_of(start + c * ch, ch), ch)

    def in_copy(c, slot):
        return pltpu.make_async_copy(xs_ref.at[rows(c), :], xbuf.at[slot], insem.at[slot])

    def out_copy(c, slot):
        return pltpu.make_async_copy(ybuf.at[slot], y_ref.at[rows(c), :], outsem.at[slot])

    @pl.when(nchunk > 0)
    def _():
        in_copy(0, 0).start(priority=CHUNK_DMA_PRIORITY)

    w1b[...] = w1_ref[0, 0].astype(BF16)
    w3b[...] = w3_ref[0, 0].astype(BF16)
    w2b[...] = w2_ref[0, 0].astype(BF16)

    def body(c, _):
        slot = c % 2

        @pl.when(c + 1 < nchunk)
        def _():
            in_copy(c + 1, 1 - slot).start(priority=CHUNK_DMA_PRIORITY)

        in_copy(c, slot).wait()

        @pl.when(c >= 2)
        def _():
            out_copy(c - 2, slot).wait()

        lo, hi = _unpack_rows(xbuf[slot])
        xb = jnp.concatenate([lo.astype(BF16), hi.astype(BF16)], axis=1)
        hid = jax.nn.silu(_dot(xb, w1b[...])) * _dot(xb, w3b[...])
        ybuf[slot] = _pack_rows(_dot(hid.astype(BF16), w2b[...]))
        out_copy(c, slot).start(priority=CHUNK_DMA_PRIORITY)
        return 0

    lax.fori_loop(0, nchunk, body, 0)

    @pl.when(nchunk >= 2)
    def _():
        out_copy(nchunk - 2, nchunk % 2).wait()

    @pl.when(nchunk >= 1)
    def _():
        out_copy(nchunk - 1, (nchunk - 1) % 2).wait()

    @pl.when(e == pl.num_programs(0) - 1)
    def _():
        ybuf[0] = jnp.zeros(ybuf.shape[1:], jnp.uint32)

        def tail_copy(b):
            return pltpu.make_async_copy(
                ybuf.at[0], y_ref.at[pl.ds(pl.multiple_of(b * ch, ch), ch), :], outsem.at[0])

        def tail_start(b, _):
            tail_copy(b).start()
            return 0

        def tail_wait(b, _):
            tail_copy(b).wait()
            return 0

        lax.fori_loop(meta_ref[0], n_blocks, tail_start, 0)
        lax.fori_loop(meta_ref[0], n_blocks, tail_wait, 0)


def _experts(layer, seg_start, seg_chunks, meta, xs, w1, w3, w2):
    n_rows = xs.shape[0]
    ch = EXPERT_BLOCK

    def w_map(e, *_):
        return (layer, e, 0, 0)

    return pl.pallas_call(
        _expert_kernel,
        grid_spec=pltpu.PrefetchScalarGridSpec(
            num_scalar_prefetch=3,
            grid=(N_EXPERTS,),
            in_specs=[
                pl.BlockSpec(memory_space=pl.ANY),
                pl.BlockSpec((1, 1, D_MODEL, D_EXPERT), w_map),
                pl.BlockSpec((1, 1, D_MODEL, D_EXPERT), w_map),
                pl.BlockSpec((1, 1, D_EXPERT, D_MODEL), w_map),
            ],
            out_specs=pl.BlockSpec(memory_space=pl.ANY),
            scratch_shapes=[
                pltpu.VMEM((2, ch, ROW_WORDS), jnp.uint32),
                pltpu.VMEM((2, ch, ROW_WORDS), jnp.uint32),
                pltpu.VMEM((D_MODEL, D_EXPERT), BF16),
                pltpu.VMEM((D_MODEL, D_EXPERT), BF16),
                pltpu.VMEM((D_EXPERT, D_MODEL), BF16),
                pltpu.SemaphoreType.DMA((2,)),
                pltpu.SemaphoreType.DMA((2,)),
            ],
        ),
        out_shape=jax.ShapeDtypeStruct((n_rows, ROW_WORDS), jnp.uint32),
        compiler_params=pltpu.CompilerParams(
            dimension_semantics=("arbitrary",), vmem_limit_bytes=VMEM_LIMIT),
        name="experts",
    )(seg_start, seg_chunks, meta, xs, w1, w3, w2)


def _combine_kernel(dest_ref, h_ref, wcol_ref, g_ref, y_ref, o_ref, buf, sem, *, final_norm):
    tc = h_ref.shape[0]
    i = pl.program_id(0)
    n = pl.num_programs(0)

    def row_copy(p, slot, k, r):
        return pltpu.make_async_copy(y_ref.at[pl.ds(p, 1), :],
                                     buf.at[2 * slot + k, pl.ds(r, 1), :], sem.at[slot])

    n_tok = n * tc

    def issue(tile, slot):
        def body(g, _):
            r0 = pl.multiple_of(g * DMA_UNROLL, DMA_UNROLL)
            for u in range(DMA_UNROLL):
                r = r0 + u
                tok = tile * tc + r
                row_copy(dest_ref[tok], slot, 0, r).start(priority=0)
                row_copy(dest_ref[n_tok + tok], slot, 1, r).start(priority=1)
            return 0
        lax.fori_loop(0, tc // DMA_UNROLL, body, 0)

    @pl.when(i == 0)
    def _():
        issue(0, 0)

    @pl.when(i + 1 < n)
    def _():
        issue(i + 1, (i + 1) % 2)

    slot = i % 2

    for k in range(2):
        pltpu.make_async_copy(y_ref.at[pl.ds(0, tc), :], buf.at[2 * slot + k], sem.at[slot]).wait()
    w = wcol_ref[...]
    lo0, hi0 = _unpack_rows(buf[2 * slot])
    lo1, hi1 = _unpack_rows(buf[2 * slot + 1])
    moe = jnp.concatenate([w[:, 0:1] * lo0 + w[:, 1:2] * lo1,
                           w[:, 0:1] * hi0 + w[:, 1:2] * hi1], axis=1)
    out = h_ref[...] + moe
    if final_norm:
        out = _rms(out, g_ref[...])
    o_ref[...] = out


def _combine(dest_flat, h, wcol, g, y, final_norm):
    t = h.shape[0]
    tc = COMBINE_TILE
    return pl.pallas_call(
        functools.partial(_combine_kernel, final_norm=final_norm),
        grid_spec=pltpu.PrefetchScalarGridSpec(
            num_scalar_prefetch=1,
            grid=(t // tc,),
            in_specs=[
                pl.BlockSpec((tc, D_MODEL), lambda i, d: (i, 0)),
                pl.BlockSpec((tc, ROUTER_ROWS), lambda i, d: (i, 0)),
                pl.BlockSpec((1, D_MODEL), lambda i, d: (0, 0)),
                pl.BlockSpec(memory_space=pl.ANY),
            ],
            out_specs=pl.BlockSpec((tc, D_MODEL), lambda i, d: (i, 0)),
            scratch_shapes=[
                pltpu.VMEM((4, tc, ROW_WORDS), jnp.uint32),
                pltpu.SemaphoreType.DMA((2,)),
            ],
        ),
        out_shape=jax.ShapeDtypeStruct((t, D_MODEL), F32),
        compiler_params=pltpu.CompilerParams(
            dimension_semantics=("arbitrary",), vmem_limit_bytes=VMEM_LIMIT),
        name="combine",
    )(dest_flat, h, wcol, g, y)


def _block_diag(w):
    g, i, o = w.shape
    eye = jnp.eye(g, dtype=w.dtype)
    return jnp.einsum("gio,gk->giko", w, eye).reshape(g * i, g * o)


def _layer_params(l, a):
    row = lambda v: v.reshape(1, -1)
    vec_rows = [a["pool_b"][l].reshape(-1), a["pool_scale"][l], a["s5_d"][l], a["s5_glu_b"][l],
                a["lru_conv_b"][l], a["lru_ba"][l].reshape(-1), a["lru_bx"][l].reshape(-1),
                a["lru_lambda"][l],
                a["conv_a_w"][l][0], a["conv_a_w"][l][1], a["conv_a_w"][l][2],
                a["lru_conv_w"][l][0], a["lru_conv_w"][l][1], a["lru_conv_w"][l][2],
                a["lru_conv_w"][l][3], jnp.zeros((MIX_W,), F32)]
    s5v = jnp.stack([a["s5_lambda_re"][l].reshape(-1), a["s5_lambda_im"][l].reshape(-1),
                     jnp.repeat(a["s5_log_step"][l], S5_STATE)]
                    + [jnp.zeros((S5_LANES,), F32)] * 5)
    b_re = _block_diag(jnp.swapaxes(a["s5_b_re"][l], 1, 2))
    b_im = _block_diag(jnp.swapaxes(a["s5_b_im"][l], 1, 2))
    c_re = _block_diag(jnp.swapaxes(a["s5_c_re"][l], 1, 2))
    c_im = _block_diag(jnp.swapaxes(a["s5_c_im"][l], 1, 2))
    wr_t = jnp.zeros((ROUTER_ROWS, D_MODEL), F32)
    wr_t = wr_t.at[0:N_GROUPS].set(a["router_group_w"][l].T)
    wr_t = wr_t.at[SUBLANES:SUBLANES + N_EXPERTS].set(a["router_expert_w"][l].T)
    br = jnp.zeros((ROUTER_ROWS,), F32)
    br = br.at[0:N_GROUPS].set(a["router_group_b"][l])
    br = br.at[SUBLANES:SUBLANES + N_EXPERTS].set(a["router_expert_b"][l])
    return {
        "g1": row(a["norm1_g"][l]),
        "w_in": a["w_in"][l].astype(BF16),
        "vec": jnp.stack(vec_rows),
        "s5v": s5v,
        "wpool": _block_diag(a["pool_w"][l]).astype(BF16),
        "bbig": jnp.concatenate([b_re, b_im], axis=1).astype(BF16),
        "cbig": jnp.concatenate([c_re, -c_im], axis=0).astype(BF16),
        "glu_w": a["s5_glu_w"][l].astype(BF16),
        "wa": _block_diag(a["lru_wa"][l]).astype(BF16),
        "wx": _block_diag(a["lru_wx"][l]).astype(BF16),
        "wg": a["merge_gate_w"][l].astype(BF16),
        "bg": row(a["merge_gate_b"][l]),
        "bw": a["branch_w"][l].astype(BF16),
        "w_out": a["w_out"][l].astype(BF16),
        "g2": row(a["norm2_g"][l]),
        "wr_t": wr_t.astype(BF16),
        "br": br.reshape(ROUTER_ROWS, 1),
    }


def _moe(layer, h, p, w1, w3, w2, g_final, final_norm):
    t = h.shape[0]
    bm = EXPERT_BLOCK
    xn, route_i, wcol, counts = _router(h, p["g2"], p["wr_t"], p["br"])
    cnt = counts[:, 0]
    padded = ((cnt + bm - 1) // bm) * bm
    pad_end = jnp.cumsum(padded)
    pad_start = pad_end - padded
    n_rows = (-(-(2 * t) // bm)) * bm + N_EXPERTS * bm
    n_blocks = n_rows // bm
    meta = (pad_end[-1:] // bm).astype(jnp.int32)
    dest = _dest(route_i, pad_start.astype(F32).reshape(N_EXPERTS, 1))
    dest_flat = dest[0:2].reshape(-1)
    xs = _dispatch(dest_flat, (pad_start + cnt).astype(jnp.int32), (padded - cnt).astype(jnp.int32),
                   meta, xn, n_rows)
    y = _experts(layer, pad_start.astype(jnp.int32), (padded // bm).astype(jnp.int32), meta,
                 xs, w1, w3, w2)
    return _combine(dest_flat, h, wcol, g_final, y, final_norm)


def kernel(x, norm1_g, w_in, conv_a_w, pool_w, pool_b, pool_scale, s5_lambda_re, s5_lambda_im, s5_log_step, s5_b_re, s5_b_im, s5_c_re, s5_c_im, s5_d, s5_glu_w, s5_glu_b, lru_conv_w, lru_conv_b, lru_wa, lru_ba, lru_wx, lru_bx, lru_lambda, merge_gate_w, merge_gate_b, branch_w, w_out, norm2_g, router_group_w, router_group_b, router_expert_w, router_expert_b, expert_w1, expert_w3, expert_w2, final_norm_g):
    a = dict(norm1_g=norm1_g, w_in=w_in, conv_a_w=conv_a_w, pool_w=pool_w, pool_b=pool_b,
             pool_scale=pool_scale, s5_lambda_re=s5_lambda_re, s5_lambda_im=s5_lambda_im,
             s5_log_step=s5_log_step, s5_b_re=s5_b_re, s5_b_im=s5_b_im, s5_c_re=s5_c_re,
             s5_c_im=s5_c_im, s5_d=s5_d, s5_glu_w=s5_glu_w, s5_glu_b=s5_glu_b,
             lru_conv_w=lru_conv_w, lru_conv_b=lru_conv_b, lru_wa=lru_wa, lru_ba=lru_ba,
             lru_wx=lru_wx, lru_bx=lru_bx, lru_lambda=lru_lambda, merge_gate_w=merge_gate_w,
             merge_gate_b=merge_gate_b, branch_w=branch_w, w_out=w_out, norm2_g=norm2_g,
             router_group_w=router_group_w, router_group_b=router_group_b,
             router_expert_w=router_expert_w, router_expert_b=router_expert_b)
    batch, seq, d = x.shape
    depth = norm1_g.shape[0]
    h = x.reshape(batch * seq, d)
    g_final = final_norm_g.reshape(1, d)
    for l in range(depth):
        p = _layer_params(l, a)
        h = _mixer(h, p, batch, seq)
        h = _moe(l, h, p, expert_w1, expert_w3, expert_w2, g_final, l == depth - 1)
    return h.reshape(batch, seq, d)
```

```python
import functools

import jax
import jax.numpy as jnp
from jax import lax
from jax.experimental import pallas as pl
from jax.experimental.pallas import tpu as pltpu

F32 = jnp.float32
BF16 = jnp.bfloat16

D_MODEL = 1024
MIX_W = 256
N_BRANCH = 4
POOL_WINDOWS = (2, 4, 8, 16)
POOL_GC = 64
S5_GROUPS = 16
S5_GROUP_CH = 16
S5_STATE = 64
S5_LANES = S5_GROUPS * S5_STATE
S5_EIG_CLIP = -1e-4
LRU_HEADS = 4
LRU_C = 8.0
IN_COLS = 1792
N_GROUPS = 4
EXP_PER_GROUP = 8
N_EXPERTS = 32
D_EXPERT = 512
EPS = 1e-6

SUBLANES = 8
LANES = 128
ROW_WORDS = D_MODEL // 2
HALO = 16
SEQ_TILE = 256
ROUTE_TILE = 512
DISPATCH_TILE = 512
COMBINE_TILE = 512
EXPERT_BLOCK = 256
ROUTER_ROWS = 128
DEST_TILE = 2048
DMA_UNROLL = 8
EXPERT_IN_BUFFERS = 3
CHUNK_DMA_PRIORITY = 1
PAD_PIECES = tuple(EXPERT_BLOCK >> (k + 1) for k in range(EXPERT_BLOCK.bit_length() - 1))
VMEM_LIMIT = 56 * 1024 * 1024

(_R_POOL_B, _R_POOL_SCALE, _R_S5_D, _R_GLU_B, _R_CONV_B, _R_BA, _R_BX, _R_LAM,
 _R_CA0, _R_CA1, _R_CA2, _R_CD0, _R_CD1, _R_CD2, _R_CD3) = range(15)


def _rms(x, g):
    return x * lax.rsqrt(jnp.mean(x * x, axis=-1, keepdims=True) + EPS) * g


def _dot(a, b):
    return jnp.dot(a, b, preferred_element_type=F32)


def _pack_rows(x):
    half = x.shape[1] // 2
    lo = lax.bitcast_convert_type(x[:, :half].astype(BF16).astype(F32), jnp.uint32)
    hi = lax.bitcast_convert_type(x[:, half:].astype(BF16).astype(F32), jnp.uint32)
    return (hi & jnp.uint32(0xFFFF0000)) | (lo >> 16)


def _unpack_rows(w):
    lo = lax.bitcast_convert_type(w << 16, F32)
    hi = lax.bitcast_convert_type(w & jnp.uint32(0xFFFF0000), F32)
    return lo, hi


def _mixer_kernel(h_ref, g1_ref, win_ref, vec_ref, s5v_ref, wpool_ref, bbig_ref, cbig_ref,
                  glu_ref, wa_ref, wx_ref, wg_ref, bg_ref, bw_ref, wout_ref,
                  o_ref,
                  ext_a, ext_b, ext_d, st_ref, su_ref, xc_ref, yc_ref, lb_ref, s5c_ref, lruc_ref,
                  are_ref, aim_ref, alre_ref, alim_ref, wre_ref, wim_ref, ptre_ref, ptim_ref,
                  coef_ref):
    ts = h_ref.shape[0]
    n_pos = ts // SUBLANES
    s = pl.program_id(1)
    row8 = lax.broadcasted_iota(jnp.int32, (SUBLANES, S5_LANES), 0)

    def put(ref, val):
        for half in range(2):
            ref[half] = val[:, half * LANES:(half + 1) * LANES]

    def get(ref):
        return jnp.concatenate([ref[0], ref[1]], axis=1)

    def to_chunk_major(ref):
        return jnp.concatenate(
            [jnp.concatenate([ref[half, pl.ds(pos, SUBLANES, stride=n_pos), :] for pos in range(n_pos)],
                             axis=0) for half in range(2)], axis=1)

    def store_time_major(ref, pos, val):
        for half in range(2):
            ref[half, pl.ds(pos, SUBLANES, stride=n_pos), :] = val[:, half * LANES:(half + 1) * LANES]

    def prow(pos):
        return slice(pos * SUBLANES, (pos + 1) * SUBLANES)

    @pl.when(s == 0)
    def _start_of_sequence():
        zeros_halo = jnp.zeros((HALO, MIX_W), F32)
        ext_a[0:HALO, :] = zeros_halo
        ext_b[0:HALO, :] = zeros_halo
        ext_d[0:HALO, :] = zeros_halo
        s5c_ref[...] = jnp.zeros(s5c_ref.shape, F32)
        lruc_ref[...] = jnp.zeros(lruc_ref.shape, F32)

    @pl.when((s == 0) & (pl.program_id(0) == 0))
    def _s5_discretisation():
        lam_re = jnp.minimum(s5v_ref[0:1, :], S5_EIG_CLIP)
        lam_im = s5v_ref[1:2, :]
        dt = jnp.exp(s5v_ref[2:3, :])
        xr = lam_re * dt
        th = lam_im * dt

        def power(k):
            ek = jnp.exp(k * xr)
            return ek * jnp.cos(k * th), ek * jnp.sin(k * th)

        a_re, a_im = power(1.0)
        den = lam_re * lam_re + lam_im * lam_im
        coef_ref[0:1, :] = ((a_re - 1.0) * lam_re + a_im * lam_im) / den
        coef_ref[1:2, :] = (a_im * lam_re - (a_re - 1.0) * lam_im) / den
        are_ref[...] = jnp.broadcast_to(a_re, are_ref.shape)
        aim_ref[...] = jnp.broadcast_to(a_im, aim_ref.shape)
        al_re, al_im = power(float(n_pos))
        alre_ref[...] = jnp.broadcast_to(al_re, alre_ref.shape)
        alim_ref[...] = jnp.broadcast_to(al_im, alim_ref.shape)
        for j, sh in enumerate((1, 2, 4)):
            p_re, p_im = power(float(sh * n_pos))
            keep = row8 >= sh
            wre_ref[prow(j), :] = jnp.where(keep, p_re, 0.0)
            wim_ref[prow(j), :] = jnp.where(keep, p_im, 0.0)
        kk = (lax.broadcasted_iota(jnp.int32, (n_pos, S5_LANES), 0) + 1).astype(F32)
        t_re, t_im = power(kk)
        for pos in range(n_pos):
            ptre_ref[prow(pos), :] = jnp.broadcast_to(t_re[pos:pos + 1, :], (SUBLANES, S5_LANES))
            ptim_ref[prow(pos), :] = jnp.broadcast_to(t_im[pos:pos + 1, :], (SUBLANES, S5_LANES))

    def vrow(r):
        return vec_ref[r:r + 1, :]

    h = h_ref[...]
    xn = _rms(h, g1_ref[...])
    xnb = xn.astype(BF16)
    proj = _dot(xnb, win_ref[...])
    a_b = proj[:, 0:256]
    a_c = proj[:, 256:512]
    a_x = proj[:, 512:768]
    p_u = proj[:, 768:1024]
    s_u = proj[:, 1024:1280]
    l_x = proj[:, 1280:1536]
    l_g = proj[:, 1536:1792]

    ext_a[HALO:HALO + ts, :] = a_c * a_x
    conv = ext_a[HALO - 2:HALO - 2 + ts, :] * vrow(_R_CA0)
    conv = conv + ext_a[HALO - 1:HALO - 1 + ts, :] * vrow(_R_CA1)
    conv = conv + ext_a[HALO:HALO + ts, :] * vrow(_R_CA2)
    ya = a_b * conv
    ext_a[0:HALO, :] = ext_a[ts:ts + HALO, :]

    ext_b[HALO:HALO + ts, :] = p_u
    lane = lax.broadcasted_iota(jnp.int32, (1, MIX_W), 1)
    grp = jnp.right_shift(lane, 6)
    win = p_u
    acc = p_u
    sh = 1
    for gi, w in enumerate(POOL_WINDOWS):
        while sh < w:
            acc = acc + ext_b[HALO - sh:HALO - sh + ts, :]
            sh += 1
        if gi > 0:
            win = jnp.where(grp >= gi, acc, win)
        else:
            win = acc
    wlane = jnp.where(grp == 0, 2.0, jnp.where(grp == 1, 4.0, jnp.where(grp == 2, 8.0, 16.0)))
    tpos = (s * ts + lax.broadcasted_iota(jnp.int32, (ts, MIX_W), 0) + 1).astype(F32)
    cnt = jnp.minimum(tpos, wlane)
    pooled = win / cnt - p_u
    yb = (_dot(pooled.astype(BF16), wpool_ref[...]) + vrow(_R_POOL_B)) * vrow(_R_POOL_SCALE)
    ext_b[0:HALO, :] = ext_b[ts:ts + HALO, :]

    put(su_ref, s_u)
    u_cm = to_chunk_major(su_ref)
    bu = _dot(u_cm.astype(BF16), bbig_ref[...])
    bre = bu[:, :S5_LANES]
    bim = bu[:, S5_LANES:]
    c_re = coef_ref[0:1, :]
    c_im = coef_ref[1:2, :]
    st_ref[:, :S5_LANES] = c_re * bre - c_im * bim
    st_ref[:, S5_LANES:] = c_re * bim + c_im * bre
    a_re = are_ref[...]
    a_im = aim_ref[...]
    hr = st_ref[prow(0), :S5_LANES]
    hi = st_ref[prow(0), S5_LANES:]
    for pos in range(1, n_pos):
        hr, hi = (a_re * hr - a_im * hi + st_ref[prow(pos), :S5_LANES],
                  a_re * hi + a_im * hr + st_ref[prow(pos), S5_LANES:])
        st_ref[prow(pos), :S5_LANES] = hr
        st_ref[prow(pos), S5_LANES:] = hi
    first = row8 == 0
    fr = jnp.where(first, s5c_ref[0:1, :S5_LANES], pltpu.roll(hr, 1, axis=0))
    fi = jnp.where(first, s5c_ref[0:1, S5_LANES:], pltpu.roll(hi, 1, axis=0))
    for jj, shift in enumerate((1, 2, 4)):
        wr = wre_ref[prow(jj), :]
        wi = wim_ref[prow(jj), :]
        sr = pltpu.roll(fr, shift, axis=0)
        si = pltpu.roll(fi, shift, axis=0)
        fr, fi = fr + (wr * sr - wi * si), fi + (wr * si + wi * sr)
    al_re = alre_ref[...]
    al_im = alim_ref[...]
    nxt_re = al_re * fr - al_im * fi + hr
    nxt_im = al_re * fi + al_im * fr + hi
    s5c_ref[0:1, :S5_LANES] = nxt_re[SUBLANES - 1:SUBLANES, :]
    s5c_ref[0:1, S5_LANES:] = nxt_im[SUBLANES - 1:SUBLANES, :]
    for pos in range(n_pos):
        pr = ptre_ref[prow(pos), :]
        pi = ptim_ref[prow(pos), :]
        st_ref[prow(pos), :S5_LANES] = st_ref[prow(pos), :S5_LANES] + (pr * fr - pi * fi)
        st_ref[prow(pos), S5_LANES:] = st_ref[prow(pos), S5_LANES:] + (pr * fi + pi * fr)
    yc = (_dot(st_ref[:, :S5_LANES].astype(BF16), cbig_ref[:S5_LANES, :])
          + _dot(st_ref[:, S5_LANES:].astype(BF16), cbig_ref[S5_LANES:, :]))
    yc = yc + vrow(_R_S5_D) * u_cm
    yc = jax.nn.gelu(yc)
    yc = yc * jax.nn.sigmoid(_dot(yc.astype(BF16), glu_ref[...]) + vrow(_R_GLU_B))
    for pos in range(n_pos):
        store_time_major(yc_ref, pos, yc[prow(pos), :])
    yc = get(yc_ref)

    ext_d[HALO:HALO + ts, :] = l_x
    xc = ext_d[HALO - 3:HALO - 3 + ts, :] * vrow(_R_CD0)
    xc = xc + ext_d[HALO - 2:HALO - 2 + ts, :] * vrow(_R_CD1)
    xc = xc + ext_d[HALO - 1:HALO - 1 + ts, :] * vrow(_R_CD2)
    xc = xc + ext_d[HALO:HALO + ts, :] * vrow(_R_CD3)
    xc = xc + vrow(_R_CONV_B)
    ext_d[0:HALO, :] = ext_d[ts:ts + HALO, :]
    put(xc_ref, xc)
    xc = to_chunk_major(xc_ref)
    xcb = xc.astype(BF16)
    r_gate = jax.nn.sigmoid(_dot(xcb, wa_ref[...]) + vrow(_R_BA))
    i_gate = jax.nn.sigmoid(_dot(xcb, wx_ref[...]) + vrow(_R_BX))
    z = -vrow(_R_LAM)
    softplus = jnp.maximum(z, 0.0) + jnp.log1p(jnp.exp(-jnp.abs(z)))
    log_a = -LRU_C * r_gate * softplus
    a_t = jnp.exp(log_a)
    mult = jnp.sqrt(1.0 - a_t * a_t)
    b_t = mult * (i_gate * xc)
    hh = b_t[prow(0), :]
    aa = a_t[prow(0), :]
    h_loc = [hh]
    a_cum = [aa]
    for pos in range(1, n_pos):
        a_pos = a_t[prow(pos), :]
        hh = a_pos * hh + b_t[prow(pos), :]
        aa = a_pos * aa
        h_loc.append(hh)
        a_cum.append(aa)
    row8w = lax.broadcasted_iota(jnp.int32, (SUBLANES, MIX_W), 0)
    f = jnp.where(row8w == 0, lruc_ref[0:1, :], pltpu.roll(hh, 1, axis=0))
    m = pltpu.roll(aa, 1, axis=0)
    for shift in (1, 2, 4):
        keep = row8w >= shift
        f_s = pltpu.roll(f, shift, axis=0)
        m_s = pltpu.roll(m, shift, axis=0)
        f = jnp.where(keep, m * f_s + f, f)
        m = jnp.where(keep, m * m_s, m)
    lruc_ref[0:1, :] = (aa * f + hh)[SUBLANES - 1:SUBLANES, :]
    for pos in range(n_pos):
        store_time_major(lb_ref, pos, h_loc[pos] + a_cum[pos] * f)
    yd = get(lb_ref) * jax.nn.gelu(l_g)

    merged = None
    for k, yk in enumerate((ya, yb, yc, yd)):
        gate = jax.nn.sigmoid(_dot(xnb, wg_ref[:, k * D_MODEL:(k + 1) * D_MODEL])
                              + bg_ref[:, k * D_MODEL:(k + 1) * D_MODEL])
        term = gate * _dot(yk.astype(BF16), bw_ref[k])
        merged = term if merged is None else merged + term
    o_ref[...] = h + _dot(merged.astype(BF16), wout_ref[...])


def _const_spec(shape):
    nd = len(shape)
    return pl.BlockSpec(shape, lambda b, s: (0,) * nd, pipeline_mode=pl.Buffered(1))


def _mixer(h, p, batch, seq):
    ts = SEQ_TILE
    ns = seq // ts
    t = batch * seq
    weights = (p["g1"], p["w_in"], p["vec"], p["s5v"], p["wpool"], p["bbig"], p["cbig"], p["glu_w"],
               p["wa"], p["wx"], p["wg"], p["bg"], p["bw"], p["w_out"])
    in_specs = [pl.BlockSpec((ts, D_MODEL), lambda b, s: (b * ns + s, 0))]
    in_specs += [_const_spec(w.shape) for w in weights]
    return pl.pallas_call(
        _mixer_kernel,
        grid=(batch, ns),
        in_specs=in_specs,
        out_specs=pl.BlockSpec((ts, D_MODEL), lambda b, s: (b * ns + s, 0)),
        out_shape=jax.ShapeDtypeStruct((t, D_MODEL), F32),
        scratch_shapes=[
            pltpu.VMEM((HALO + ts, MIX_W), F32),
            pltpu.VMEM((HALO + ts, MIX_W), F32),
            pltpu.VMEM((HALO + ts, MIX_W), F32),
            pltpu.VMEM((ts, 2 * S5_LANES), F32),
            pltpu.VMEM((2, ts, LANES), F32),
            pltpu.VMEM((2, ts, LANES), F32),
            pltpu.VMEM((2, ts, LANES), F32),
            pltpu.VMEM((2, ts, LANES), F32),
            pltpu.VMEM((SUBLANES, 2 * S5_LANES), F32),
            pltpu.VMEM((SUBLANES, MIX_W), F32),
            pltpu.VMEM((SUBLANES, S5_LANES), F32),
            pltpu.VMEM((SUBLANES, S5_LANES), F32),
            pltpu.VMEM((SUBLANES, S5_LANES), F32),
            pltpu.VMEM((SUBLANES, S5_LANES), F32),
            pltpu.VMEM((3 * SUBLANES, S5_LANES), F32),
            pltpu.VMEM((3 * SUBLANES, S5_LANES), F32),
            pltpu.VMEM((ts, S5_LANES), F32),
            pltpu.VMEM((ts, S5_LANES), F32),
            pltpu.VMEM((SUBLANES, S5_LANES), F32),
        ],
        compiler_params=pltpu.CompilerParams(
            dimension_semantics=("arbitrary", "arbitrary"),
            vmem_limit_bytes=VMEM_LIMIT),
        name="mixer",
    )(h, *weights)


def _router_kernel(h_ref, g2_ref, wr_ref, br_ref, xn_ref, ri_ref, wcol_ref, cnt_ref, carry_ref):
    tm = h_ref.shape[0]
    i = pl.program_id(0)

    @pl.when(i == 0)
    def _():
        carry_ref[...] = jnp.zeros(carry_ref.shape, F32)

    xn = _rms(h_ref[...], g2_ref[...])
    xn_ref[...] = _pack_rows(xn)
    logits = lax.dot_general(wr_ref[...], xn.astype(BF16), (((1,), (1,)), ((), ())),
                             preferred_element_type=F32) + br_ref[...]
    row8 = lax.broadcasted_iota(jnp.int32, (SUBLANES, tm), 0)
    neg_inf = jnp.float32(-jnp.inf)
    gl = jnp.where(row8 < N_GROUPS, logits[0:SUBLANES, :], neg_inf)
    gmax = jnp.max(gl, axis=0, keepdims=True)
    ge = jnp.exp(gl - gmax)
    gp = ge / jnp.sum(ge, axis=0, keepdims=True)
    g_val = jnp.max(gp, axis=0, keepdims=True)
    g_idx = jnp.min(jnp.where(gp == g_val, row8, SUBLANES), axis=0, keepdims=True)
    sel = logits[4 * SUBLANES:5 * SUBLANES, :]
    for g in (2, 1, 0):
        sel = jnp.where(g_idx == g, logits[(g + 1) * SUBLANES:(g + 2) * SUBLANES, :], sel)
    v1 = jnp.max(sel, axis=0, keepdims=True)
    i1 = jnp.min(jnp.where(sel == v1, row8, SUBLANES), axis=0, keepdims=True)
    sel2 = jnp.where(row8 == i1, neg_inf, sel)
    v2 = jnp.max(sel2, axis=0, keepdims=True)
    i2 = jnp.min(jnp.where(sel2 == v2, row8, SUBLANES), axis=0, keepdims=True)
    e2 = jnp.exp(v2 - v1)
    denom = 1.0 + e2
    w1 = (1.0 / denom) * g_val
    w2 = (e2 / denom) * g_val
    eid0 = g_idx * EXP_PER_GROUP + i1
    eid1 = g_idx * EXP_PER_GROUP + i2
    e32 = lax.broadcasted_iota(jnp.int32, (N_EXPERTS, tm), 0)
    oh0 = (e32 == eid0).astype(F32)
    oh1 = (e32 == eid1).astype(F32)
    oh = oh0 + oh1
    before = (lax.broadcasted_iota(jnp.int32, (tm, tm), 0)
              < lax.broadcasted_iota(jnp.int32, (tm, tm), 1)).astype(BF16)
    base = _dot(oh.astype(BF16), before) + carry_ref[:, 0:1]
    rank0 = jnp.sum(oh0 * base, axis=0, keepdims=True)
    rank1 = jnp.sum(oh1 * base, axis=0, keepdims=True)
    new_carry = carry_ref[...] + jnp.sum(oh, axis=1, keepdims=True)
    carry_ref[...] = new_carry
    cnt_ref[...] = new_carry.astype(jnp.int32)
    ri_ref[...] = jnp.zeros(ri_ref.shape, jnp.int32)
    ri_ref[0:1, :] = eid0
    ri_ref[1:2, :] = eid1
    ri_ref[2:3, :] = rank0.astype(jnp.int32)
    ri_ref[3:4, :] = rank1.astype(jnp.int32)
    rows = lax.broadcasted_iota(jnp.int32, (ROUTER_ROWS, tm), 0)
    wrows = jnp.where(rows == 0, w1, jnp.where(rows == 1, w2, 0.0))
    wcol_ref[...] = wrows.T


def _router(h, g2, wr_t, br_col):
    t = h.shape[0]
    tm = ROUTE_TILE
    return pl.pallas_call(
        _router_kernel,
        grid=(t // tm,),
        in_specs=[
            pl.BlockSpec((tm, D_MODEL), lambda i: (i, 0)),
            pl.BlockSpec((1, D_MODEL), lambda i: (0, 0)),
            pl.BlockSpec((ROUTER_ROWS, D_MODEL), lambda i: (0, 0)),
            pl.BlockSpec((ROUTER_ROWS, 1), lambda i: (0, 0)),
        ],
        out_specs=[
            pl.BlockSpec((tm, ROW_WORDS), lambda i: (i, 0)),
            pl.BlockSpec((SUBLANES, tm), lambda i: (0, i)),
            pl.BlockSpec((tm, ROUTER_ROWS), lambda i: (i, 0)),
            pl.BlockSpec((N_EXPERTS, 128), lambda i: (0, 0)),
        ],
        out_shape=[
            jax.ShapeDtypeStruct((t, ROW_WORDS), jnp.uint32),
            jax.ShapeDtypeStruct((SUBLANES, t), jnp.int32),
            jax.ShapeDtypeStruct((t, ROUTER_ROWS), F32),
            jax.ShapeDtypeStruct((N_EXPERTS, 128), jnp.int32),
        ],
        scratch_shapes=[pltpu.VMEM((N_EXPERTS, 128), F32)],
        compiler_params=pltpu.CompilerParams(
            dimension_semantics=("arbitrary",), vmem_limit_bytes=VMEM_LIMIT),
        name="router",
    )(h, g2, wr_t, br_col)


def _dest_kernel(ri_ref, ps_ref, o_ref):
    tm = ri_ref.shape[1]
    e32 = lax.broadcasted_iota(jnp.int32, (N_EXPERTS, tm), 0)
    o_ref[...] = jnp.zeros(o_ref.shape, jnp.int32)
    for k in range(2):
        start = jnp.sum(jnp.where(e32 == ri_ref[k:k + 1, :], ps_ref[...], 0.0), axis=0, keepdims=True)
        o_ref[k:k + 1, :] = start.astype(jnp.int32) + ri_ref[2 + k:3 + k, :]


def _dest(route_i, pad_start_col):
    t = route_i.shape[1]
    tm = DEST_TILE
    return pl.pallas_call(
        _dest_kernel,
        grid=(t // tm,),
        in_specs=[
            pl.BlockSpec((SUBLANES, tm), lambda i: (0, i)),
            pl.BlockSpec((N_EXPERTS, 1), lambda i: (0, 0)),
        ],
        out_specs=pl.BlockSpec((SUBLANES, tm), lambda i: (0, i)),
        out_shape=jax.ShapeDtypeStruct((SUBLANES, t), jnp.int32),
        compiler_params=pltpu.CompilerParams(dimension_semantics=("arbitrary",)),
        name="dest",
    )(route_i, pad_start_col)


def _dispatch_kernel(dest_ref, fs_ref, fl_ref, meta_ref, x_ref, xs_ref, zbuf, sem, zsem):
    tm = x_ref.shape[0]
    bm = zbuf.shape[0]
    n_blocks = xs_ref.shape[0] // bm
    i = pl.program_id(0)
    n_tok = pl.num_programs(0) * tm
    base = i * tm

    def pad_copy(off, p):
        return pltpu.make_async_copy(zbuf.at[pl.ds(0, p), :], xs_ref.at[pl.ds(off, p), :], zsem)

    def for_each_fill(act):
        def segment(e, _):
            off = fs_ref[e]
            n = fl_ref[e]
            head = n & (SUBLANES - 1)
            for j in range(SUBLANES - 1):
                @pl.when(j < head)
                def _(off=off, j=j):
                    act(pad_copy(off + j, 1))

            off = pl.multiple_of(off + head, SUBLANES)
            for p in PAD_PIECES:
                if p < SUBLANES:
                    continue
                piece = n & p

                @pl.when(piece != 0)
                def _(off=off, p=p):
                    act(pad_copy(off, p))

                off = pl.multiple_of(off + piece, SUBLANES)
            return 0

        lax.fori_loop(0, N_EXPERTS, segment, 0)

        def tail(b, _):
            act(pad_copy(pl.multiple_of(b * bm, bm), bm))
            return 0

        lax.fori_loop(meta_ref[0], n_blocks, tail, 0)

    @pl.when(i == 0)
    def _():
        zbuf[...] = jnp.zeros(zbuf.shape, jnp.uint32)
        for_each_fill(lambda c: c.start())
        for_each_fill(lambda c: c.wait())

    def row_copy(r, d):
        return pltpu.make_async_copy(x_ref.at[pl.ds(r, 1), :], xs_ref.at[pl.ds(d, 1), :], sem)

    def issue(g, _):
        r0 = pl.multiple_of(g * DMA_UNROLL, DMA_UNROLL)
        for u in range(DMA_UNROLL):
            r = r0 + u
            row_copy(r, dest_ref[base + r]).start(priority=0)
            row_copy(r, dest_ref[n_tok + base + r]).start(priority=1)
        return 0

    lax.fori_loop(0, tm // DMA_UNROLL, issue, 0)

    for _ in range(2):
        pltpu.make_async_copy(x_ref, xs_ref.at[pl.ds(0, tm), :], sem).wait()


def _dispatch(dest_flat, fill_start, fill_len, meta, xn, n_rows):
    t = xn.shape[0]
    tm = DISPATCH_TILE
    return pl.pallas_call(
        _dispatch_kernel,
        grid_spec=pltpu.PrefetchScalarGridSpec(
            num_scalar_prefetch=4,
            grid=(t // tm,),
            in_specs=[pl.BlockSpec((tm, ROW_WORDS), lambda i, *_: (i, 0))],
            out_specs=pl.BlockSpec(memory_space=pl.ANY),
            scratch_shapes=[
                pltpu.VMEM((EXPERT_BLOCK, ROW_WORDS), jnp.uint32),
                pltpu.SemaphoreType.DMA(()),
                pltpu.SemaphoreType.DMA(()),
            ],
        ),
        out_shape=jax.ShapeDtypeStruct((n_rows, ROW_WORDS), jnp.uint32),
        compiler_params=pltpu.CompilerParams(dimension_semantics=("arbitrary",)),
        name="dispatch",
    )(dest_flat, fill_start, fill_len, meta, xn)


def _expert_kernel(ss_ref, nc_ref, meta_ref, xs_ref, w1_ref, w3_ref, w2_ref, y_ref,
                   xbuf, ybuf, w1b, w3b, w2b, insem, outsem):
    e = pl.program_id(0)
    ch = xbuf.shape[1]
    n_blocks = y_ref.shape[0] // ch
    start = ss_ref[e]
    nchunk = nc_ref[e]

    n_in = xbuf.shape[0]
    n_exp = pl.num_programs(0)

    def rows(seg_start, c):
        return pl.ds(pl.multiple_of(seg_start + c * ch, ch), ch)

    def in_copy(seg_start, c, slot):
        return pltpu.make_async_copy(xs_ref.at[rows(seg_start, c), :], xbuf.at[slot], insem.at[slot])

    def out_copy(c, slot):
        return pltpu.make_async_copy(ybuf.at[slot], y_ref.at[rows(start, c), :], outsem.at[slot])

    def prime(seg_start, seg_chunks):
        for k in range(n_in):
            @pl.when(k < seg_chunks)
            def _(k=k):
                in_copy(seg_start, k, k).start(priority=CHUNK_DMA_PRIORITY)

    @pl.when(e == 0)
    def _():
        prime(start, nchunk)

    w1b[...] = w1_ref[0, 0].astype(BF16)
    w3b[...] = w3_ref[0, 0].astype(BF16)
    w2b[...] = w2_ref[0, 0].astype(BF16)

    def body(c, _):
        slot = c % 2
        islot = c % n_in
        in_copy(start, c, islot).wait()

        @pl.when(c >= 2)
        def _():
            out_copy(c - 2, slot).wait()

        lo, hi = _unpack_rows(xbuf[islot])
        xb = jnp.concatenate([lo.astype(BF16), hi.astype(BF16)], axis=1)
        hid = jax.nn.silu(_dot(xb, w1b[...])) * _dot(xb, w3b[...])
        ybuf[slot] = _pack_rows(_dot(hid.astype(BF16), w2b[...]))
        out_copy(c, slot).start(priority=CHUNK_DMA_PRIORITY)

        @pl.when(c + n_in < nchunk)
        def _():
            in_copy(start, c + n_in, islot).start(priority=CHUNK_DMA_PRIORITY)

        return 0

    lax.fori_loop(0, nchunk, body, 0)

    @pl.when(e + 1 < n_exp)
    def _():
        nxt = jnp.minimum(e + 1, n_exp - 1)
        prime(ss_ref[nxt], nc_ref[nxt])

    @pl.when(nchunk >= 2)
    def _():
        out_copy(nchunk - 2, nchunk % 2).wait()

    @pl.when(nchunk >= 1)
    def _():
        out_copy(nchunk - 1, (nchunk - 1) % 2).wait()

    @pl.when(e == pl.num_programs(0) - 1)
    def _():
        ybuf[0] = jnp.zeros(ybuf.shape[1:], jnp.uint32)

        def tail_copy(b):
            return pltpu.make_async_copy(
                ybuf.at[0], y_ref.at[pl.ds(pl.multiple_of(b * ch, ch), ch), :], outsem.at[0])

        def tail_start(b, _):
            tail_copy(b).start()
            return 0

        def tail_wait(b, _):
            tail_copy(b).wait()
            return 0

        lax.fori_loop(meta_ref[0], n_blocks, tail_start, 0)
        lax.fori_loop(meta_ref[0], n_blocks, tail_wait, 0)


def _experts(layer, seg_start, seg_chunks, meta, xs, w1, w3, w2):
    n_rows = xs.shape[0]
    ch = EXPERT_BLOCK

    def w_map(e, *_):
        return (layer, e, 0, 0)

    return pl.pallas_call(
        _expert_kernel,
        grid_spec=pltpu.PrefetchScalarGridSpec(
            num_scalar_prefetch=3,
            grid=(N_EXPERTS,),
            in_specs=[
                pl.BlockSpec(memory_space=pl.ANY),
                pl.BlockSpec((1, 1, D_MODEL, D_EXPERT), w_map),
                pl.BlockSpec((1, 1, D_MODEL, D_EXPERT), w_map),
                pl.BlockSpec((1, 1, D_EXPERT, D_MODEL), w_map),
            ],
            out_specs=pl.BlockSpec(memory_space=pl.ANY),
            scratch_shapes=[
                pltpu.VMEM((EXPERT_IN_BUFFERS, ch, ROW_WORDS), jnp.uint32),
                pltpu.VMEM((2, ch, ROW_WORDS), jnp.uint32),
                pltpu.VMEM((D_MODEL, D_EXPERT), BF16),
                pltpu.VMEM((D_MODEL, D_EXPERT), BF16),
                pltpu.VMEM((D_EXPERT, D_MODEL), BF16),
                pltpu.SemaphoreType.DMA((EXPERT_IN_BUFFERS,)),
                pltpu.SemaphoreType.DMA((2,)),
            ],
        ),
        out_shape=jax.ShapeDtypeStruct((n_rows, ROW_WORDS), jnp.uint32),
        compiler_params=pltpu.CompilerParams(
            dimension_semantics=("arbitrary",), vmem_limit_bytes=VMEM_LIMIT),
        name="experts",
    )(seg_start, seg_chunks, meta, xs, w1, w3, w2)


def _combine_kernel(dest_ref, h_ref, wcol_ref, g_ref, y_ref, o_ref, buf, sem, *, final_norm):
    tc = h_ref.shape[0]
    i = pl.program_id(0)
    n = pl.num_programs(0)

    def row_copy(p, slot, k, r):
        return pltpu.make_async_copy(y_ref.at[pl.ds(p, 1), :],
                                     buf.at[2 * slot + k, pl.ds(r, 1), :], sem.at[slot])

    n_tok = n * tc

    def issue(tile, slot):
        def body(g, _):
            r0 = pl.multiple_of(g * DMA_UNROLL, DMA_UNROLL)
            for u in range(DMA_UNROLL):
                r = r0 + u
                tok = tile * tc + r
                row_copy(dest_ref[tok], slot, 0, r).start(priority=0)
                row_copy(dest_ref[n_tok + tok], slot, 1, r).start(priority=1)
            return 0
        lax.fori_loop(0, tc // DMA_UNROLL, body, 0)

    @pl.when(i == 0)
    def _():
        issue(0, 0)

    @pl.when(i + 1 < n)
    def _():
        issue(i + 1, (i + 1) % 2)

    slot = i % 2

    for k in range(2):
        pltpu.make_async_copy(y_ref.at[pl.ds(0, tc), :], buf.at[2 * slot + k], sem.at[slot]).wait()
    w = wcol_ref[...]
    lo0, hi0 = _unpack_rows(buf[2 * slot])
    lo1, hi1 = _unpack_rows(buf[2 * slot + 1])
    moe = jnp.concatenate([w[:, 0:1] * lo0 + w[:, 1:2] * lo1,
                           w[:, 0:1] * hi0 + w[:, 1:2] * hi1], axis=1)
    out = h_ref[...] + moe
    if final_norm:
        out = _rms(out, g_ref[...])
    o_ref[...] = out


def _combine(dest_flat, h, wcol, g, y, final_norm):
    t = h.shape[0]
    tc = COMBINE_TILE
    return pl.pallas_call(
        functools.partial(_combine_kernel, final_norm=final_norm),
        grid_spec=pltpu.PrefetchScalarGridSpec(
            num_scalar_prefetch=1,
            grid=(t // tc,),
            in_specs=[
                pl.BlockSpec((tc, D_MODEL), lambda i, d: (i, 0)),
                pl.BlockSpec((tc, ROUTER_ROWS), lambda i, d: (i, 0)),
                pl.BlockSpec((1, D_MODEL), lambda i, d: (0, 0)),
                pl.BlockSpec(memory_space=pl.ANY),
            ],
            out_specs=pl.BlockSpec((tc, D_MODEL), lambda i, d: (i, 0)),
            scratch_shapes=[
                pltpu.VMEM((4, tc, ROW_WORDS), jnp.uint32),
                pltpu.SemaphoreType.DMA((2,)),
            ],
        ),
        out_shape=jax.ShapeDtypeStruct((t, D_MODEL), F32),
        compiler_params=pltpu.CompilerParams(
            dimension_semantics=("arbitrary",), vmem_limit_bytes=VMEM_LIMIT),
        name="combine",
    )(dest_flat, h, wcol, g, y)


def _block_diag(w):
    g, i, o = w.shape
    eye = jnp.eye(g, dtype=w.dtype)
    return jnp.einsum("gio,gk->giko", w, eye).reshape(g * i, g * o)


def _layer_params(l, a):
    row = lambda v: v.reshape(1, -1)
    vec_rows = [a["pool_b"][l].reshape(-1), a["pool_scale"][l], a["s5_d"][l], a["s5_glu_b"][l],
                a["lru_conv_b"][l], a["lru_ba"][l].reshape(-1), a["lru_bx"][l].reshape(-1),
                a["lru_lambda"][l],
                a["conv_a_w"][l][0], a["conv_a_w"][l][1], a["conv_a_w"][l][2],
                a["lru_conv_w"][l][0], a["lru_conv_w"][l][1], a["lru_conv_w"][l][2],
                a["lru_conv_w"][l][3], jnp.zeros((MIX_W,), F32)]
    s5v = jnp.stack([a["s5_lambda_re"][l].reshape(-1), a["s5_lambda_im"][l].reshape(-1),
                     jnp.repeat(a["s5_log_step"][l], S5_STATE)]
                    + [jnp.zeros((S5_LANES,), F32)] * 5)
    b_re = _block_diag(jnp.swapaxes(a["s5_b_re"][l], 1, 2))
    b_im = _block_diag(jnp.swapaxes(a["s5_b_im"][l], 1, 2))
    c_re = _block_diag(jnp.swapaxes(a["s5_c_re"][l], 1, 2))
    c_im = _block_diag(jnp.swapaxes(a["s5_c_im"][l], 1, 2))
    wr_t = jnp.zeros((ROUTER_ROWS, D_MODEL), F32)
    wr_t = wr_t.at[0:N_GROUPS].set(a["router_group_w"][l].T)
    wr_t = wr_t.at[SUBLANES:SUBLANES + N_EXPERTS].set(a["router_expert_w"][l].T)
    br = jnp.zeros((ROUTER_ROWS,), F32)
    br = br.at[0:N_GROUPS].set(a["router_group_b"][l])
    br = br.at[SUBLANES:SUBLANES + N_EXPERTS].set(a["router_expert_b"][l])
    return {
        "g1": row(a["norm1_g"][l]),
        "w_in": a["w_in"][l].astype(BF16),
        "vec": jnp.stack(vec_rows),
        "s5v": s5v,
        "wpool": _block_diag(a["pool_w"][l]).astype(BF16),
        "bbig": jnp.concatenate([b_re, b_im], axis=1).astype(BF16),
        "cbig": jnp.concatenate([c_re, -c_im], axis=0).astype(BF16),
        "glu_w": a["s5_glu_w"][l].astype(BF16),
        "wa": _block_diag(a["lru_wa"][l]).astype(BF16),
        "wx": _block_diag(a["lru_wx"][l]).astype(BF16),
        "wg": a["merge_gate_w"][l].astype(BF16),
        "bg": row(a["merge_gate_b"][l]),
        "bw": a["branch_w"][l].astype(BF16),
        "w_out": a["w_out"][l].astype(BF16),
        "g2": row(a["norm2_g"][l]),
        "wr_t": wr_t.astype(BF16),
        "br": br.reshape(ROUTER_ROWS, 1),
    }


def _moe(layer, h, p, w1, w3, w2, g_final, final_norm):
    t = h.shape[0]
    bm = EXPERT_BLOCK
    xn, route_i, wcol, counts = _router(h, p["g2"], p["wr_t"], p["br"])
    cnt = counts[:, 0]
    padded = ((cnt + bm - 1) // bm) * bm
    pad_end = jnp.cumsum(padded)
    pad_start = pad_end - padded
    n_rows = (-(-(2 * t) // bm)) * bm + N_EXPERTS * bm
    n_blocks = n_rows // bm
    meta = (pad_end[-1:] // bm).astype(jnp.int32)
    dest = _dest(route_i, pad_start.astype(F32).reshape(N_EXPERTS, 1))
    dest_flat = dest[0:2].reshape(-1)
    xs = _dispatch(dest_flat, (pad_start + cnt).astype(jnp.int32), (padded - cnt).astype(jnp.int32),
                   meta, xn, n_rows)
    y = _experts(layer, pad_start.astype(jnp.int32), (padded // bm).astype(jnp.int32), meta,
                 xs, w1, w3, w2)
    return _combine(dest_flat, h, wcol, g_final, y, final_norm)


def kernel(x, norm1_g, w_in, conv_a_w, pool_w, pool_b, pool_scale, s5_lambda_re, s5_lambda_im, s5_log_step, s5_b_re, s5_b_im, s5_c_re, s5_c_im, s5_d, s5_glu_w, s5_glu_b, lru_conv_w, lru_conv_b, lru_wa, lru_ba, lru_wx, lru_bx, lru_lambda, merge_gate_w, merge_gate_b, branch_w, w_out, norm2_g, router_group_w, router_group_b, router_expert_w, router_expert_b, expert_w1, expert_w3, expert_w2, final_norm_g):
    a = dict(norm1_g=norm1_g, w_in=w_in, conv_a_w=conv_a_w, pool_w=pool_w, pool_b=pool_b,
             pool_scale=pool_scale, s5_lambda_re=s5_lambda_re, s5_lambda_im=s5_lambda_im,
             s5_log_step=s5_log_step, s5_b_re=s5_b_re, s5_b_im=s5_b_im, s5_c_re=s5_c_re,
             s5_c_im=s5_c_im, s5_d=s5_d, s5_glu_w=s5_glu_w, s5_glu_b=s5_glu_b,
             lru_conv_w=lru_conv_w, lru_conv_b=lru_conv_b, lru_wa=lru_wa, lru_ba=lru_ba,
             lru_wx=lru_wx, lru_bx=lru_bx, lru_lambda=lru_lambda, merge_gate_w=merge_gate_w,
             merge_gate_b=merge_gate_b, branch_w=branch_w, w_out=w_out, norm2_g=norm2_g,
             router_group_w=router_group_w, router_group_b=router_group_b,
             router_expert_w=router_expert_w, router_expert_b=router_expert_b)
    batch, seq, d = x.shape
    depth = norm1_g.shape[0]
    h = x.reshape(batch * seq, d)
    g_final = final_norm_g.reshape(1, d)
    for l in range(depth):
        p = _layer_params(l, a)
        h = _mixer(h, p, batch, seq)
        h = _moe(l, h, p, expert_w1, expert_w3, expert_w2, g_final, l == depth - 1)
    return h.reshape(batch, seq, d)
```

```python
import functools

import jax
import jax.numpy as jnp
from jax import lax
from jax.experimental import pallas as pl
from jax.experimental.pallas import tpu as pltpu

F32 = jnp.float32
BF16 = jnp.bfloat16

D_MODEL = 1024
MIX_W = 256
N_BRANCH = 4
POOL_WINDOWS = (2, 4, 8, 16)
POOL_GC = 64
S5_GROUPS = 16
S5_GROUP_CH = 16
S5_STATE = 64
S5_LANES = S5_GROUPS * S5_STATE
S5_EIG_CLIP = -1e-4
LRU_HEADS = 4
LRU_C = 8.0
IN_COLS = 1792
N_GROUPS = 4
EXP_PER_GROUP = 8
N_EXPERTS = 32
D_EXPERT = 512
EPS = 1e-6

SUBLANES = 8
LANES = 128
ROW_WORDS = D_MODEL // 2
HALO = 16
SEQ_TILE = 256
ROUTE_TILE = 512
DISPATCH_TILE = 512
COMBINE_TILE = 512
EXPERT_BLOCK = 256
ROUTER_ROWS = 128
DEST_TILE = 2048
EXPERT_IN_BUFFERS = 3
CHUNK_DMA_PRIORITY = 1
PAD_PIECES = tuple(EXPERT_BLOCK >> (k + 1) for k in range(EXPERT_BLOCK.bit_length() - 1))
VMEM_LIMIT = 56 * 1024 * 1024

(_R_POOL_B, _R_POOL_SCALE, _R_S5_D, _R_GLU_B, _R_CONV_B, _R_BA, _R_BX, _R_LAM,
 _R_CA0, _R_CA1, _R_CA2, _R_CD0, _R_CD1, _R_CD2, _R_CD3) = range(15)


def _rms(x, g):
    return x * lax.rsqrt(jnp.mean(x * x, axis=-1, keepdims=True) + EPS) * g


def _dot(a, b):
    return jnp.dot(a, b, preferred_element_type=F32)


def _pack_rows(x):
    half = x.shape[1] // 2
    lo = lax.bitcast_convert_type(x[:, :half].astype(BF16).astype(F32), jnp.uint32)
    hi = lax.bitcast_convert_type(x[:, half:].astype(BF16).astype(F32), jnp.uint32)
    return (hi & jnp.uint32(0xFFFF0000)) | (lo >> 16)


def _unpack_rows(w):
    lo = lax.bitcast_convert_type(w << 16, F32)
    hi = lax.bitcast_convert_type(w & jnp.uint32(0xFFFF0000), F32)
    return lo, hi


def _mixer_kernel(h_ref, g1_ref, win_ref, vec_ref, s5v_ref, wpool_ref, bbig_ref, cbig_ref,
                  glu_ref, wa_ref, wx_ref, wg_ref, bg_ref, bw_ref, wout_ref,
                  o_ref,
                  ext_a, ext_b, ext_d, st_ref, su_ref, xc_ref, yc_ref, lb_ref, s5c_ref, lruc_ref,
                  are_ref, aim_ref, alre_ref, alim_ref, wre_ref, wim_ref, ptre_ref, ptim_ref,
                  coef_ref):
    ts = h_ref.shape[0]
    n_pos = ts // SUBLANES
    s = pl.program_id(1)
    row8 = lax.broadcasted_iota(jnp.int32, (SUBLANES, S5_LANES), 0)

    def put(ref, val):
        for half in range(2):
            ref[half] = val[:, half * LANES:(half + 1) * LANES]

    def get(ref):
        return jnp.concatenate([ref[0], ref[1]], axis=1)

    def to_chunk_major(ref):
        return jnp.concatenate(
            [jnp.concatenate([ref[half, pl.ds(pos, SUBLANES, stride=n_pos), :] for pos in range(n_pos)],
                             axis=0) for half in range(2)], axis=1)

    def store_time_major(ref, pos, val):
        for half in range(2):
            ref[half, pl.ds(pos, SUBLANES, stride=n_pos), :] = val[:, half * LANES:(half + 1) * LANES]

    def prow(pos):
        return slice(pos * SUBLANES, (pos + 1) * SUBLANES)

    @pl.when(s == 0)
    def _start_of_sequence():
        zeros_halo = jnp.zeros((HALO, MIX_W), F32)
        ext_a[0:HALO, :] = zeros_halo
        ext_b[0:HALO, :] = zeros_halo
        ext_d[0:HALO, :] = zeros_halo
        s5c_ref[...] = jnp.zeros(s5c_ref.shape, F32)
        lruc_ref[...] = jnp.zeros(lruc_ref.shape, F32)

    @pl.when((s == 0) & (pl.program_id(0) == 0))
    def _s5_discretisation():
        lam_re = jnp.minimum(s5v_ref[0:1, :], S5_EIG_CLIP)
        lam_im = s5v_ref[1:2, :]
        dt = jnp.exp(s5v_ref[2:3, :])
        xr = lam_re * dt
        th = lam_im * dt

        def power(k):
            ek = jnp.exp(k * xr)
            return ek * jnp.cos(k * th), ek * jnp.sin(k * th)

        a_re, a_im = power(1.0)
        den = lam_re * lam_re + lam_im * lam_im
        coef_ref[0:1, :] = ((a_re - 1.0) * lam_re + a_im * lam_im) / den
        coef_ref[1:2, :] = (a_im * lam_re - (a_re - 1.0) * lam_im) / den
        are_ref[...] = jnp.broadcast_to(a_re, are_ref.shape)
        aim_ref[...] = jnp.broadcast_to(a_im, aim_ref.shape)
        al_re, al_im = power(float(n_pos))
        alre_ref[...] = jnp.broadcast_to(al_re, alre_ref.shape)
        alim_ref[...] = jnp.broadcast_to(al_im, alim_ref.shape)
        for j, sh in enumerate((1, 2, 4)):
            p_re, p_im = power(float(sh * n_pos))
            keep = row8 >= sh
            wre_ref[prow(j), :] = jnp.where(keep, p_re, 0.0)
            wim_ref[prow(j), :] = jnp.where(keep, p_im, 0.0)
        kk = (lax.broadcasted_iota(jnp.int32, (n_pos, S5_LANES), 0) + 1).astype(F32)
        t_re, t_im = power(kk)
        for pos in range(n_pos):
            ptre_ref[prow(pos), :] = jnp.broadcast_to(t_re[pos:pos + 1, :], (SUBLANES, S5_LANES))
            ptim_ref[prow(pos), :] = jnp.broadcast_to(t_im[pos:pos + 1, :], (SUBLANES, S5_LANES))

    def vrow(r):
        return vec_ref[r:r + 1, :]

    h = h_ref[...]
    xn = _rms(h, g1_ref[...])
    xnb = xn.astype(BF16)
    proj = _dot(xnb, win_ref[...])
    a_b = proj[:, 0:256]
    a_c = proj[:, 256:512]
    a_x = proj[:, 512:768]
    p_u = proj[:, 768:1024]
    s_u = proj[:, 1024:1280]
    l_x = proj[:, 1280:1536]
    l_g = proj[:, 1536:1792]

    def gate(k):
        return jax.nn.sigmoid(_dot(xnb, wg_ref[:, k * D_MODEL:(k + 1) * D_MODEL])
                              + bg_ref[:, k * D_MODEL:(k + 1) * D_MODEL])

    gates = [gate(0)]

    ext_a[HALO:HALO + ts, :] = a_c * a_x
    conv = ext_a[HALO - 2:HALO - 2 + ts, :] * vrow(_R_CA0)
    conv = conv + ext_a[HALO - 1:HALO - 1 + ts, :] * vrow(_R_CA1)
    conv = conv + ext_a[HALO:HALO + ts, :] * vrow(_R_CA2)
    ya = a_b * conv
    ext_a[0:HALO, :] = ext_a[ts:ts + HALO, :]

    ext_b[HALO:HALO + ts, :] = p_u
    lane = lax.broadcasted_iota(jnp.int32, (1, MIX_W), 1)
    grp = jnp.right_shift(lane, 6)
    win = p_u
    acc = p_u
    sh = 1
    for gi, w in enumerate(POOL_WINDOWS):
        while sh < w:
            acc = acc + ext_b[HALO - sh:HALO - sh + ts, :]
            sh += 1
        if gi > 0:
            win = jnp.where(grp >= gi, acc, win)
        else:
            win = acc
    wlane = jnp.where(grp == 0, 2.0, jnp.where(grp == 1, 4.0, jnp.where(grp == 2, 8.0, 16.0)))
    tpos = (s * ts + lax.broadcasted_iota(jnp.int32, (ts, MIX_W), 0) + 1).astype(F32)
    cnt = jnp.minimum(tpos, wlane)
    pooled = win / cnt - p_u
    yb = (_dot(pooled.astype(BF16), wpool_ref[...]) + vrow(_R_POOL_B)) * vrow(_R_POOL_SCALE)
    ext_b[0:HALO, :] = ext_b[ts:ts + HALO, :]

    put(su_ref, s_u)
    u_cm = to_chunk_major(su_ref)
    bu = _dot(u_cm.astype(BF16), bbig_ref[...])
    bre = bu[:, :S5_LANES]
    bim = bu[:, S5_LANES:]
    c_re = coef_ref[0:1, :]
    c_im = coef_ref[1:2, :]
    st_ref[:, :S5_LANES] = c_re * bre - c_im * bim
    st_ref[:, S5_LANES:] = c_re * bim + c_im * bre
    gates.append(gate(1))
    a_re = are_ref[...]
    a_im = aim_ref[...]
    hr = st_ref[prow(0), :S5_LANES]
    hi = st_ref[prow(0), S5_LANES:]
    for pos in range(1, n_pos):
        hr, hi = (a_re * hr - a_im * hi + st_ref[prow(pos), :S5_LANES],
                  a_re * hi + a_im * hr + st_ref[prow(pos), S5_LANES:])
        st_ref[prow(pos), :S5_LANES] = hr
        st_ref[prow(pos), S5_LANES:] = hi
    first = row8 == 0
    fr = jnp.where(first, s5c_ref[0:1, :S5_LANES], pltpu.roll(hr, 1, axis=0))
    fi = jnp.where(first, s5c_ref[0:1, S5_LANES:], pltpu.roll(hi, 1, axis=0))
    for jj, shift in enumerate((1, 2, 4)):
        wr = wre_ref[prow(jj), :]
        wi = wim_ref[prow(jj), :]
        sr = pltpu.roll(fr, shift, axis=0)
        si = pltpu.roll(fi, shift, axis=0)
        fr, fi = fr + (wr * sr - wi * si), fi + (wr * si + wi * sr)
    al_re = alre_ref[...]
    al_im = alim_ref[...]
    nxt_re = al_re * fr - al_im * fi + hr
    nxt_im = al_re * fi + al_im * fr + hi
    s5c_ref[0:1, :S5_LANES] = nxt_re[SUBLANES - 1:SUBLANES, :]
    s5c_ref[0:1, S5_LANES:] = nxt_im[SUBLANES - 1:SUBLANES, :]
    gates.append(gate(2))
    for pos in range(n_pos):
        pr = ptre_ref[prow(pos), :]
        pi = ptim_ref[prow(pos), :]
        st_ref[prow(pos), :S5_LANES] = st_ref[prow(pos), :S5_LANES] + (pr * fr - pi * fi)
        st_ref[prow(pos), S5_LANES:] = st_ref[prow(pos), S5_LANES:] + (pr * fi + pi * fr)
    yc = (_dot(st_ref[:, :S5_LANES].astype(BF16), cbig_ref[:S5_LANES, :])
          + _dot(st_ref[:, S5_LANES:].astype(BF16), cbig_ref[S5_LANES:, :]))
    yc = yc + vrow(_R_S5_D) * u_cm
    yc = jax.nn.gelu(yc)
    yc = yc * jax.nn.sigmoid(_dot(yc.astype(BF16), glu_ref[...]) + vrow(_R_GLU_B))
    for pos in range(n_pos):
        store_time_major(yc_ref, pos, yc[prow(pos), :])
    yc = get(yc_ref)

    gates.append(gate(3))

    ext_d[HALO:HALO + ts, :] = l_x
    xc = ext_d[HALO - 3:HALO - 3 + ts, :] * vrow(_R_CD0)
    xc = xc + ext_d[HALO - 2:HALO - 2 + ts, :] * vrow(_R_CD1)
    xc = xc + ext_d[HALO - 1:HALO - 1 + ts, :] * vrow(_R_CD2)
    xc = xc + ext_d[HALO:HALO + ts, :] * vrow(_R_CD3)
    xc = xc + vrow(_R_CONV_B)
    ext_d[0:HALO, :] = ext_d[ts:ts + HALO, :]
    put(xc_ref, xc)
    xc = to_chunk_major(xc_ref)
    xcb = xc.astype(BF16)
    r_gate = jax.nn.sigmoid(_dot(xcb, wa_ref[...]) + vrow(_R_BA))
    i_gate = jax.nn.sigmoid(_dot(xcb, wx_ref[...]) + vrow(_R_BX))
    z = -vrow(_R_LAM)
    softplus = jnp.maximum(z, 0.0) + jnp.log1p(jnp.exp(-jnp.abs(z)))
    log_a = -LRU_C * r_gate * softplus
    a_t = jnp.exp(log_a)
    mult = jnp.sqrt(1.0 - a_t * a_t)
    b_t = mult * (i_gate * xc)
    hh = b_t[prow(0), :]
    aa = a_t[prow(0), :]
    h_loc = [hh]
    a_cum = [aa]
    for pos in range(1, n_pos):
        a_pos = a_t[prow(pos), :]
        hh = a_pos * hh + b_t[prow(pos), :]
        aa = a_pos * aa
        h_loc.append(hh)
        a_cum.append(aa)
    row8w = lax.broadcasted_iota(jnp.int32, (SUBLANES, MIX_W), 0)
    f = jnp.where(row8w == 0, lruc_ref[0:1, :], pltpu.roll(hh, 1, axis=0))
    m = pltpu.roll(aa, 1, axis=0)
    for shift in (1, 2, 4):
        keep = row8w >= shift
        f_s = pltpu.roll(f, shift, axis=0)
        m_s = pltpu.roll(m, shift, axis=0)
        f = jnp.where(keep, m * f_s + f, f)
        m = jnp.where(keep, m * m_s, m)
    lruc_ref[0:1, :] = (aa * f + hh)[SUBLANES - 1:SUBLANES, :]
    for pos in range(n_pos):
        store_time_major(lb_ref, pos, h_loc[pos] + a_cum[pos] * f)
    yd = get(lb_ref) * jax.nn.gelu(l_g)

    merged = None
    for k, yk in enumerate((ya, yb, yc, yd)):
        term = gates[k] * _dot(yk.astype(BF16), bw_ref[k])
        merged = term if merged is None else merged + term
    o_ref[...] = h + _dot(merged.astype(BF16), wout_ref[...])


def _const_spec(shape):
    nd = len(shape)
    return pl.BlockSpec(shape, lambda b, s: (0,) * nd, pipeline_mode=pl.Buffered(1))


def _mixer(h, p, batch, seq):
    ts = SEQ_TILE
    ns = seq // ts
    t = batch * seq
    weights = (p["g1"], p["w_in"], p["vec"], p["s5v"], p["wpool"], p["bbig"], p["cbig"], p["glu_w"],
               p["wa"], p["wx"], p["wg"], p["bg"], p["bw"], p["w_out"])
    in_specs = [pl.BlockSpec((ts, D_MODEL), lambda b, s: (b * ns + s, 0))]
    in_specs += [_const_spec(w.shape) for w in weights]
    return pl.pallas_call(
        _mixer_kernel,
        grid=(batch, ns),
        in_specs=in_specs,
        out_specs=pl.BlockSpec((ts, D_MODEL), lambda b, s: (b * ns + s, 0)),
        out_shape=jax.ShapeDtypeStruct((t, D_MODEL), F32),
        scratch_shapes=[
            pltpu.VMEM((HALO + ts, MIX_W), F32),
            pltpu.VMEM((HALO + ts, MIX_W), F32),
            pltpu.VMEM((HALO + ts, MIX_W), F32),
            pltpu.VMEM((ts, 2 * S5_LANES), F32),
            pltpu.VMEM((2, ts, LANES), F32),
            pltpu.VMEM((2, ts, LANES), F32),
            pltpu.VMEM((2, ts, LANES), F32),
            pltpu.VMEM((2, ts, LANES), F32),
            pltpu.VMEM((SUBLANES, 2 * S5_LANES), F32),
            pltpu.VMEM((SUBLANES, MIX_W), F32),
            pltpu.VMEM((SUBLANES, S5_LANES), F32),
            pltpu.VMEM((SUBLANES, S5_LANES), F32),
            pltpu.VMEM((SUBLANES, S5_LANES), F32),
            pltpu.VMEM((SUBLANES, S5_LANES), F32),
            pltpu.VMEM((3 * SUBLANES, S5_LANES), F32),
            pltpu.VMEM((3 * SUBLANES, S5_LANES), F32),
            pltpu.VMEM((ts, S5_LANES), F32),
            pltpu.VMEM((ts, S5_LANES), F32),
            pltpu.VMEM((SUBLANES, S5_LANES), F32),
        ],
        compiler_params=pltpu.CompilerParams(
            dimension_semantics=("arbitrary", "arbitrary"),
            vmem_limit_bytes=VMEM_LIMIT),
        name="mixer",
    )(h, *weights)


def _router_kernel(h_ref, g2_ref, wr_ref, br_ref, xn_ref, ri_ref, wcol_ref, cnt_ref, carry_ref):
    tm = h_ref.shape[0]
    i = pl.program_id(0)

    @pl.when(i == 0)
    def _():
        carry_ref[...] = jnp.zeros(carry_ref.shape, F32)

    xn = _rms(h_ref[...], g2_ref[...])
    xn_ref[...] = _pack_rows(xn)
    logits = lax.dot_general(wr_ref[...], xn.astype(BF16), (((1,), (1,)), ((), ())),
                             preferred_element_type=F32) + br_ref[...]
    row8 = lax.broadcasted_iota(jnp.int32, (SUBLANES, tm), 0)
    neg_inf = jnp.float32(-jnp.inf)
    gl = jnp.where(row8 < N_GROUPS, logits[0:SUBLANES, :], neg_inf)
    gmax = jnp.max(gl, axis=0, keepdims=True)
    ge = jnp.exp(gl - gmax)
    gp = ge / jnp.sum(ge, axis=0, keepdims=True)
    g_val = jnp.max(gp, axis=0, keepdims=True)
    g_idx = jnp.min(jnp.where(gp == g_val, row8, SUBLANES), axis=0, keepdims=True)
    sel = logits[4 * SUBLANES:5 * SUBLANES, :]
    for g in (2, 1, 0):
        sel = jnp.where(g_idx == g, logits[(g + 1) * SUBLANES:(g + 2) * SUBLANES, :], sel)
    v1 = jnp.max(sel, axis=0, keepdims=True)
    i1 = jnp.min(jnp.where(sel == v1, row8, SUBLANES), axis=0, keepdims=True)
    sel2 = jnp.where(row8 == i1, neg_inf, sel)
    v2 = jnp.max(sel2, axis=0, keepdims=True)
    i2 = jnp.min(jnp.where(sel2 == v2, row8, SUBLANES), axis=0, keepdims=True)
    e2 = jnp.exp(v2 - v1)
    denom = 1.0 + e2
    w1 = (1.0 / denom) * g_val
    w2 = (e2 / denom) * g_val
    eid0 = g_idx * EXP_PER_GROUP + i1
    eid1 = g_idx * EXP_PER_GROUP + i2
    e32 = lax.broadcasted_iota(jnp.int32, (N_EXPERTS, tm), 0)
    oh0 = (e32 == eid0).astype(F32)
    oh1 = (e32 == eid1).astype(F32)
    oh = oh0 + oh1
    before = (lax.broadcasted_iota(jnp.int32, (tm, tm), 0)
              < lax.broadcasted_iota(jnp.int32, (tm, tm), 1)).astype(BF16)
    base = _dot(oh.astype(BF16), before) + carry_ref[:, 0:1]
    rank0 = jnp.sum(oh0 * base, axis=0, keepdims=True)
    rank1 = jnp.sum(oh1 * base, axis=0, keepdims=True)
    new_carry = carry_ref[...] + jnp.sum(oh, axis=1, keepdims=True)
    carry_ref[...] = new_carry
    cnt_ref[...] = new_carry.astype(jnp.int32)
    ri_ref[...] = jnp.zeros(ri_ref.shape, jnp.int32)
    ri_ref[0:1, :] = eid0
    ri_ref[1:2, :] = eid1
    ri_ref[2:3, :] = rank0.astype(jnp.int32)
    ri_ref[3:4, :] = rank1.astype(jnp.int32)
    rows = lax.broadcasted_iota(jnp.int32, (ROUTER_ROWS, tm), 0)
    wrows = jnp.where(rows == 0, w1, jnp.where(rows == 1, w2, 0.0))
    wcol_ref[...] = wrows.T


def _router(h, g2, wr_t, br_col):
    t = h.shape[0]
    tm = ROUTE_TILE
    return pl.pallas_call(
        _router_kernel,
        grid=(t // tm,),
        in_specs=[
            pl.BlockSpec((tm, D_MODEL), lambda i: (i, 0)),
            pl.BlockSpec((1, D_MODEL), lambda i: (0, 0)),
            pl.BlockSpec((ROUTER_ROWS, D_MODEL), lambda i: (0, 0)),
            pl.BlockSpec((ROUTER_ROWS, 1), lambda i: (0, 0)),
        ],
        out_specs=[
            pl.BlockSpec((tm, ROW_WORDS), lambda i: (i, 0)),
            pl.BlockSpec((SUBLANES, tm), lambda i: (0, i)),
            pl.BlockSpec((tm, ROUTER_ROWS), lambda i: (i, 0)),
            pl.BlockSpec((N_EXPERTS, 128), lambda i: (0, 0)),
        ],
        out_shape=[
            jax.ShapeDtypeStruct((t, ROW_WORDS), jnp.uint32),
            jax.ShapeDtypeStruct((SUBLANES, t), jnp.int32),
            jax.ShapeDtypeStruct((t, ROUTER_ROWS), F32),
            jax.ShapeDtypeStruct((N_EXPERTS, 128), jnp.int32),
        ],
        scratch_shapes=[pltpu.VMEM((N_EXPERTS, 128), F32)],
        compiler_params=pltpu.CompilerParams(
            dimension_semantics=("arbitrary",), vmem_limit_bytes=VMEM_LIMIT),
        name="router",
    )(h, g2, wr_t, br_col)


def _dest_kernel(ri_ref, ps_ref, o_ref):
    tm = ri_ref.shape[1]
    e32 = lax.broadcasted_iota(jnp.int32, (N_EXPERTS, tm), 0)
    o_ref[...] = jnp.zeros(o_ref.shape, jnp.int32)
    for k in range(2):
        start = jnp.sum(jnp.where(e32 == ri_ref[k:k + 1, :], ps_ref[...], 0.0), axis=0, keepdims=True)
        o_ref[k:k + 1, :] = start.astype(jnp.int32) + ri_ref[2 + k:3 + k, :]


def _dest(route_i, pad_start_col):
    t = route_i.shape[1]
    tm = DEST_TILE
    return pl.pallas_call(
        _dest_kernel,
        grid=(t // tm,),
        in_specs=[
            pl.BlockSpec((SUBLANES, tm), lambda i: (0, i)),
            pl.BlockSpec((N_EXPERTS, 1), lambda i: (0, 0)),
        ],
        out_specs=pl.BlockSpec((SUBLANES, tm), lambda i: (0, i)),
        out_shape=jax.ShapeDtypeStruct((SUBLANES, t), jnp.int32),
        compiler_params=pltpu.CompilerParams(dimension_semantics=("arbitrary",)),
        name="dest",
    )(route_i, pad_start_col)


def _dispatch_kernel(dest_ref, fs_ref, fl_ref, meta_ref, x_ref, xs_ref, zbuf, sem, zsem):
    tm = x_ref.shape[0]
    bm = zbuf.shape[0]
    n_blocks = xs_ref.shape[0] // bm
    i = pl.program_id(0)
    n_tok = pl.num_programs(0) * tm
    base = i * tm

    def pad_copy(off, p):
        return pltpu.make_async_copy(zbuf.at[pl.ds(0, p), :], xs_ref.at[pl.ds(off, p), :], zsem)

    def for_each_fill(act):
        def segment(e, _):
            off = fs_ref[e]
            n = fl_ref[e]
            head = n & (SUBLANES - 1)
            for j in range(SUBLANES - 1):
                @pl.when(j < head)
                def _(off=off, j=j):
                    act(pad_copy(off + j, 1))

            off = pl.multiple_of(off + head, SUBLANES)
            for p in PAD_PIECES:
                if p < SUBLANES:
                    continue
                piece = n & p

                @pl.when(piece != 0)
                def _(off=off, p=p):
                    act(pad_copy(off, p))

                off = pl.multiple_of(off + piece, SUBLANES)
            return 0

        lax.fori_loop(0, N_EXPERTS, segment, 0)

        def tail(b, _):
            act(pad_copy(pl.multiple_of(b * bm, bm), bm))
            return 0

        lax.fori_loop(meta_ref[0], n_blocks, tail, 0)

    @pl.when(i == 0)
    def _():
        zbuf[...] = jnp.zeros(zbuf.shape, jnp.uint32)
        for_each_fill(lambda c: c.start())
        for_each_fill(lambda c: c.wait())

    def row_copy(r, d):
        return pltpu.make_async_copy(x_ref.at[pl.ds(r, 1), :], xs_ref.at[pl.ds(d, 1), :], sem)

    for r in range(tm):
        row_copy(r, dest_ref[base + r]).start(priority=0)
        row_copy(r, dest_ref[n_tok + base + r]).start(priority=1)

    for _ in range(2):
        pltpu.make_async_copy(x_ref, xs_ref.at[pl.ds(0, tm), :], sem).wait()


def _dispatch(dest_flat, fill_start, fill_len, meta, xn, n_rows):
    t = xn.shape[0]
    tm = DISPATCH_TILE
    return pl.pallas_call(
        _dispatch_kernel,
        grid_spec=pltpu.PrefetchScalarGridSpec(
            num_scalar_prefetch=4,
            grid=(t // tm,),
            in_specs=[pl.BlockSpec((tm, ROW_WORDS), lambda i, *_: (i, 0))],
            out_specs=pl.BlockSpec(memory_space=pl.ANY),
            scratch_shapes=[
                pltpu.VMEM((EXPERT_BLOCK, ROW_WORDS), jnp.uint32),
                pltpu.SemaphoreType.DMA(()),
                pltpu.SemaphoreType.DMA(()),
            ],
        ),
        out_shape=jax.ShapeDtypeStruct((n_rows, ROW_WORDS), jnp.uint32),
        compiler_params=pltpu.CompilerParams(dimension_semantics=("arbitrary",)),
        name="dispatch",
    )(dest_flat, fill_start, fill_len, meta, xn)


def _expert_kernel(ss_ref, nc_ref, meta_ref, xs_ref, w1_ref, w3_ref, w2_ref, y_ref,
                   xbuf, ybuf, w1b, w3b, w2b, insem, outsem):
    e = pl.program_id(0)
    ch = xbuf.shape[1]
    n_blocks = y_ref.shape[0] // ch
    start = ss_ref[e]
    nchunk = nc_ref[e]

    n_in = xbuf.shape[0]
    n_exp = pl.num_programs(0)

    def rows(seg_start, c):
        return pl.ds(pl.multiple_of(seg_start + c * ch, ch), ch)

    def in_copy(seg_start, c, slot):
        return pltpu.make_async_copy(xs_ref.at[rows(seg_start, c), :], xbuf.at[slot], insem.at[slot])

    def out_copy(c, slot):
        return pltpu.make_async_copy(ybuf.at[slot], y_ref.at[rows(start, c), :], outsem.at[slot])

    def prime(seg_start, seg_chunks):
        for k in range(n_in):
            @pl.when(k < seg_chunks)
            def _(k=k):
                in_copy(seg_start, k, k).start(priority=CHUNK_DMA_PRIORITY)

    @pl.when(e == 0)
    def _():
        prime(start, nchunk)

    w1b[...] = w1_ref[0, 0].astype(BF16)
    w3b[...] = w3_ref[0, 0].astype(BF16)
    w2b[...] = w2_ref[0, 0].astype(BF16)

    def body(c, _):
        slot = c % 2
        islot = c % n_in
        in_copy(start, c, islot).wait()

        @pl.when(c >= 2)
        def _():
            out_copy(c - 2, slot).wait()

        lo, hi = _unpack_rows(xbuf[islot])
        xb = jnp.concatenate([lo.astype(BF16), hi.astype(BF16)], axis=1)
        hid = jax.nn.silu(_dot(xb, w1b[...])) * _dot(xb, w3b[...])
        ybuf[slot] = _pack_rows(_dot(hid.astype(BF16), w2b[...]))
        out_copy(c, slot).start(priority=CHUNK_DMA_PRIORITY)

        @pl.when(c + n_in < nchunk)
        def _():
            in_copy(start, c + n_in, islot).start(priority=CHUNK_DMA_PRIORITY)

        return 0

    lax.fori_loop(0, nchunk, body, 0)

    @pl.when(e + 1 < n_exp)
    def _():
        nxt = jnp.minimum(e + 1, n_exp - 1)
        prime(ss_ref[nxt], nc_ref[nxt])

    @pl.when(nchunk >= 2)
    def _():
        out_copy(nchunk - 2, nchunk % 2).wait()

    @pl.when(nchunk >= 1)
    def _():
        out_copy(nchunk - 1, (nchunk - 1) % 2).wait()

    @pl.when(e == pl.num_programs(0) - 1)
    def _():
        ybuf[0] = jnp.zeros(ybuf.shape[1:], jnp.uint32)

        def tail_copy(b):
            return pltpu.make_async_copy(
                ybuf.at[0], y_ref.at[pl.ds(pl.multiple_of(b * ch, ch), ch), :], outsem.at[0])

        def tail_start(b, _):
            tail_copy(b).start()
            return 0

        def tail_wait(b, _):
            tail_copy(b).wait()
            return 0

        lax.fori_loop(meta_ref[0], n_blocks, tail_start, 0)
        lax.fori_loop(meta_ref[0], n_blocks, tail_wait, 0)


def _experts(layer, seg_start, seg_chunks, meta, xs, w1, w3, w2):
    n_rows = xs.shape[0]
    ch = EXPERT_BLOCK

    def w_map(e, *_):
        return (layer, e, 0, 0)

    return pl.pallas_call(
        _expert_kernel,
        grid_spec=pltpu.PrefetchScalarGridSpec(
            num_scalar_prefetch=3,
            grid=(N_EXPERTS,),
            in_specs=[
                pl.BlockSpec(memory_space=pl.ANY),
                pl.BlockSpec((1, 1, D_MODEL, D_EXPERT), w_map),
                pl.BlockSpec((1, 1, D_MODEL, D_EXPERT), w_map),
                pl.BlockSpec((1, 1, D_EXPERT, D_MODEL), w_map),
            ],
            out_specs=pl.BlockSpec(memory_space=pl.ANY),
            scratch_shapes=[
                pltpu.VMEM((EXPERT_IN_BUFFERS, ch, ROW_WORDS), jnp.uint32),
                pltpu.VMEM((2, ch, ROW_WORDS), jnp.uint32),
                pltpu.VMEM((D_MODEL, D_EXPERT), BF16),
                pltpu.VMEM((D_MODEL, D_EXPERT), BF16),
                pltpu.VMEM((D_EXPERT, D_MODEL), BF16),
                pltpu.SemaphoreType.DMA((EXPERT_IN_BUFFERS,)),
                pltpu.SemaphoreType.DMA((2,)),
            ],
        ),
        out_shape=jax.ShapeDtypeStruct((n_rows, ROW_WORDS), jnp.uint32),
        compiler_params=pltpu.CompilerParams(
            dimension_semantics=("arbitrary",), vmem_limit_bytes=VMEM_LIMIT),
        name="experts",
    )(seg_start, seg_chunks, meta, xs, w1, w3, w2)


def _combine_kernel(dest_ref, h_ref, wcol_ref, g_ref, y_ref, o_ref, buf, sem, *, final_norm):
    tc = h_ref.shape[0]
    i = pl.program_id(0)
    n = pl.num_programs(0)

    def row_copy(p, slot, k, r):
        return pltpu.make_async_copy(y_ref.at[pl.ds(p, 1), :],
                                     buf.at[2 * slot + k, pl.ds(r, 1), :], sem.at[slot])

    n_tok = n * tc

    def issue(tile, slot):
        tok0 = tile * tc
        for r in range(tc):
            row_copy(dest_ref[tok0 + r], slot, 0, r).start(priority=0)
            row_copy(dest_ref[n_tok + tok0 + r], slot, 1, r).start(priority=1)

    @pl.when(i == 0)
    def _():
        issue(0, 0)

    @pl.when(i + 1 < n)
    def _():
        issue(i + 1, (i + 1) % 2)

    slot = i % 2

    for k in range(2):
        pltpu.make_async_copy(y_ref.at[pl.ds(0, tc), :], buf.at[2 * slot + k], sem.at[slot]).wait()
    w = wcol_ref[...]
    lo0, hi0 = _unpack_rows(buf[2 * slot])
    lo1, hi1 = _unpack_rows(buf[2 * slot + 1])
    moe = jnp.concatenate([w[:, 0:1] * lo0 + w[:, 1:2] * lo1,
                           w[:, 0:1] * hi0 + w[:, 1:2] * hi1], axis=1)
    out = h_ref[...] + moe
    if final_norm:
        out = _rms(out, g_ref[...])
    o_ref[...] = out


def _combine(dest_flat, h, wcol, g, y, final_norm):
    t = h.shape[0]
    tc = COMBINE_TILE
    return pl.pallas_call(
        functools.partial(_combine_kernel, final_norm=final_norm),
        grid_spec=pltpu.PrefetchScalarGridSpec(
            num_scalar_prefetch=1,
            grid=(t // tc,),
            in_specs=[
                pl.BlockSpec((tc, D_MODEL), lambda i, d: (i, 0)),
                pl.BlockSpec((tc, ROUTER_ROWS), lambda i, d: (i, 0)),
                pl.BlockSpec((1, D_MODEL), lambda i, d: (0, 0)),
                pl.BlockSpec(memory_space=pl.ANY),
            ],
            out_specs=pl.BlockSpec((tc, D_MODEL), lambda i, d: (i, 0)),
            scratch_shapes=[
                pltpu.VMEM((4, tc, ROW_WORDS), jnp.uint32),
                pltpu.SemaphoreType.DMA((2,)),
            ],
        ),
        out_shape=jax.ShapeDtypeStruct((t, D_MODEL), F32),
        compiler_params=pltpu.CompilerParams(
            dimension_semantics=("arbitrary",), vmem_limit_bytes=VMEM_LIMIT),
        name="combine",
    )(dest_flat, h, wcol, g, y)


def _block_diag(w):
    g, i, o = w.shape
    eye = jnp.eye(g, dtype=w.dtype)
    return jnp.einsum("gio,gk->giko", w, eye).reshape(g * i, g * o)


def _layer_params(l, a):
    row = lambda v: v.reshape(1, -1)
    vec_rows = [a["pool_b"][l].reshape(-1), a["pool_scale"][l], a["s5_d"][l], a["s5_glu_b"][l],
                a["lru_conv_b"][l], a["lru_ba"][l].reshape(-1), a["lru_bx"][l].reshape(-1),
                a["lru_lambda"][l],
                a["conv_a_w"][l][0], a["conv_a_w"][l][1], a["conv_a_w"][l][2],
                a["lru_conv_w"][l][0], a["lru_conv_w"][l][1], a["lru_conv_w"][l][2],
                a["lru_conv_w"][l][3], jnp.zeros((MIX_W,), F32)]
    s5v = jnp.stack([a["s5_lambda_re"][l].reshape(-1), a["s5_lambda_im"][l].reshape(-1),
                     jnp.repeat(a["s5_log_step"][l], S5_STATE)]
                    + [jnp.zeros((S5_LANES,), F32)] * 5)
    b_re = _block_diag(jnp.swapaxes(a["s5_b_re"][l], 1, 2))
    b_im = _block_diag(jnp.swapaxes(a["s5_b_im"][l], 1, 2))
    c_re = _block_diag(jnp.swapaxes(a["s5_c_re"][l], 1, 2))
    c_im = _block_diag(jnp.swapaxes(a["s5_c_im"][l], 1, 2))
    wr_t = jnp.zeros((ROUTER_ROWS, D_MODEL), F32)
    wr_t = wr_t.at[0:N_GROUPS].set(a["router_group_w"][l].T)
    wr_t = wr_t.at[SUBLANES:SUBLANES + N_EXPERTS].set(a["router_expert_w"][l].T)
    br = jnp.zeros((ROUTER_ROWS,), F32)
    br = br.at[0:N_GROUPS].set(a["router_group_b"][l])
    br = br.at[SUBLANES:SUBLANES + N_EXPERTS].set(a["router_expert_b"][l])
    return {
        "g1": row(a["norm1_g"][l]),
        "w_in": a["w_in"][l].astype(BF16),
        "vec": jnp.stack(vec_rows),
        "s5v": s5v,
        "wpool": _block_diag(a["pool_w"][l]).astype(BF16),
        "bbig": jnp.concatenate([b_re, b_im], axis=1).astype(BF16),
        "cbig": jnp.concatenate([c_re, -c_im], axis=0).astype(BF16),
        "glu_w": a["s5_glu_w"][l].astype(BF16),
        "wa": _block_diag(a["lru_wa"][l]).astype(BF16),
        "wx": _block_diag(a["lru_wx"][l]).astype(BF16),
        "wg": a["merge_gate_w"][l].astype(BF16),
        "bg": row(a["merge_gate_b"][l]),
        "bw": a["branch_w"][l].astype(BF16),
        "w_out": a["w_out"][l].astype(BF16),
        "g2": row(a["norm2_g"][l]),
        "wr_t": wr_t.astype(BF16),
        "br": br.reshape(ROUTER_ROWS, 1),
    }


def _moe(layer, h, p, w1, w3, w2, g_final, final_norm):
    t = h.shape[0]
    bm = EXPERT_BLOCK
    xn, route_i, wcol, counts = _router(h, p["g2"], p["wr_t"], p["br"])
    cnt = counts[:, 0]
    padded = ((cnt + bm - 1) // bm) * bm
    pad_end = jnp.cumsum(padded)
    pad_start = pad_end - padded
    n_rows = (-(-(2 * t) // bm)) * bm + N_EXPERTS * bm
    n_blocks = n_rows // bm
    meta = (pad_end[-1:] // bm).astype(jnp.int32)
    dest = _dest(route_i, pad_start.astype(F32).reshape(N_EXPERTS, 1))
    dest_flat = dest[0:2].reshape(-1)
    xs = _dispatch(dest_flat, (pad_start + cnt).astype(jnp.int32), (padded - cnt).astype(jnp.int32),
                   meta, xn, n_rows)
    y = _experts(layer, pad_start.astype(jnp.int32), (padded // bm).astype(jnp.int32), meta,
                 xs, w1, w3, w2)
    return _combine(dest_flat, h, wcol, g_final, y, final_norm)


def kernel(x, norm1_g, w_in, conv_a_w, pool_w, pool_b, pool_scale, s5_lambda_re, s5_lambda_im, s5_log_step, s5_b_re, s5_b_im, s5_c_re, s5_c_im, s5_d, s5_glu_w, s5_glu_b, lru_conv_w, lru_conv_b, lru_wa, lru_ba, lru_wx, lru_bx, lru_lambda, merge_gate_w, merge_gate_b, branch_w, w_out, norm2_g, router_group_w, router_group_b, router_expert_w, router_expert_b, expert_w1, expert_w3, expert_w2, final_norm_g):
    a = dict(norm1_g=norm1_g, w_in=w_in, conv_a_w=conv_a_w, pool_w=pool_w, pool_b=pool_b,
             pool_scale=pool_scale, s5_lambda_re=s5_lambda_re, s5_lambda_im=s5_lambda_im,
             s5_log_step=s5_log_step, s5_b_re=s5_b_re, s5_b_im=s5_b_im, s5_c_re=s5_c_re,
             s5_c_im=s5_c_im, s5_d=s5_d, s5_glu_w=s5_glu_w, s5_glu_b=s5_glu_b,
             lru_conv_w=lru_conv_w, lru_conv_b=lru_conv_b, lru_wa=lru_wa, lru_ba=lru_ba,
             lru_wx=lru_wx, lru_bx=lru_bx, lru_lambda=lru_lambda, merge_gate_w=merge_gate_w,
             merge_gate_b=merge_gate_b, branch_w=branch_w, w_out=w_out, norm2_g=norm2_g,
             router_group_w=router_group_w, router_group_b=router_group_b,
             router_expert_w=router_expert_w, router_expert_b=router_expert_b)
    batch, seq, d = x.shape
    depth = norm1_g.shape[0]
    h = x.reshape(batch * seq, d)
    g_final = final_norm_g.reshape(1, d)
    for l in range(depth):
        p = _layer_params(l, a)
        h = _mixer(h, p, batch, seq)
        h = _moe(l, h, p, expert_w1, expert_w3, expert_w2, g_final, l == depth - 1)
    return h.reshape(batch, seq, d)
```

```python
import functools

import jax
import jax.numpy as jnp
from jax import lax
from jax.experimental import pallas as pl
from jax.experimental.pallas import tpu as pltpu

F32 = jnp.float32
BF16 = jnp.bfloat16

D_MODEL = 1024
MIX_W = 256
N_BRANCH = 4
POOL_WINDOWS = (2, 4, 8, 16)
POOL_GC = 64
S5_GROUPS = 16
S5_GROUP_CH = 16
S5_STATE = 64
S5_LANES = S5_GROUPS * S5_STATE
S5_EIG_CLIP = -1e-4
LRU_HEADS = 4
LRU_C = 8.0
IN_COLS = 1792
N_GROUPS = 4
EXP_PER_GROUP = 8
N_EXPERTS = 32
D_EXPERT = 512
EPS = 1e-6

SUBLANES = 8
LANES = 128
ROW_WORDS = D_MODEL // 2
HALO = 16
SEQ_TILE = 512
ROUTE_TILE = 512
DISPATCH_TILE = 512
COMBINE_TILE = 512
EXPERT_BLOCK = 256
ROUTER_ROWS = 128
DEST_TILE = 2048
EXPERT_IN_BUFFERS = 3
CHUNK_DMA_PRIORITY = 1
PAD_PIECES = tuple(EXPERT_BLOCK >> (k + 1) for k in range(EXPERT_BLOCK.bit_length() - 1))
VMEM_LIMIT = 56 * 1024 * 1024

(_R_POOL_B, _R_POOL_SCALE, _R_S5_D, _R_GLU_B, _R_CONV_B, _R_BA, _R_BX, _R_LAM,
 _R_CA0, _R_CA1, _R_CA2, _R_CD0, _R_CD1, _R_CD2, _R_CD3) = range(15)


def _rms(x, g):
    return x * lax.rsqrt(jnp.mean(x * x, axis=-1, keepdims=True) + EPS) * g


def _dot(a, b):
    return jnp.dot(a, b, preferred_element_type=F32)


def _pack_rows(x):
    half = x.shape[1] // 2
    lo = lax.bitcast_convert_type(x[:, :half].astype(BF16).astype(F32), jnp.uint32)
    hi = lax.bitcast_convert_type(x[:, half:].astype(BF16).astype(F32), jnp.uint32)
    return (hi & jnp.uint32(0xFFFF0000)) | (lo >> 16)


def _unpack_rows(w):
    lo = lax.bitcast_convert_type(w << 16, F32)
    hi = lax.bitcast_convert_type(w & jnp.uint32(0xFFFF0000), F32)
    return lo, hi


def _mixer_kernel(h_ref, g1_ref, win_ref, vec_ref, s5v_ref, wpool_ref, bbig_ref, cbig_ref,
                  glu_ref, wa_ref, wx_ref, wg_ref, bg_ref, bw_ref, wout_ref,
                  o_ref,
                  ext_a, ext_b, ext_d, st_ref, su_ref, xc_ref, yc_ref, lb_ref, s5c_ref, lruc_ref,
                  are_ref, aim_ref, alre_ref, alim_ref, wre_ref, wim_ref, ptre_ref, ptim_ref,
                  coef_ref):
    ts = h_ref.shape[0]
    n_pos = ts // SUBLANES
    s = pl.program_id(1)
    row8 = lax.broadcasted_iota(jnp.int32, (SUBLANES, S5_LANES), 0)

    def put(ref, val):
        for half in range(2):
            ref[half] = val[:, half * LANES:(half + 1) * LANES]

    def get(ref):
        return jnp.concatenate([ref[0], ref[1]], axis=1)

    def to_chunk_major(ref):
        return jnp.concatenate(
            [jnp.concatenate([ref[half, pl.ds(pos, SUBLANES, stride=n_pos), :] for pos in range(n_pos)],
                             axis=0) for half in range(2)], axis=1)

    def store_time_major(ref, pos, val):
        for half in range(2):
            ref[half, pl.ds(pos, SUBLANES, stride=n_pos), :] = val[:, half * LANES:(half + 1) * LANES]

    def prow(pos):
        return slice(pos * SUBLANES, (pos + 1) * SUBLANES)

    @pl.when(s == 0)
    def _start_of_sequence():
        zeros_halo = jnp.zeros((HALO, MIX_W), F32)
        ext_a[0:HALO, :] = zeros_halo
        ext_b[0:HALO, :] = zeros_halo
        ext_d[0:HALO, :] = zeros_halo
        s5c_ref[...] = jnp.zeros(s5c_ref.shape, F32)
        lruc_ref[...] = jnp.zeros(lruc_ref.shape, F32)

    @pl.when((s == 0) & (pl.program_id(0) == 0))
    def _s5_discretisation():
        lam_re = jnp.minimum(s5v_ref[0:1, :], S5_EIG_CLIP)
        lam_im = s5v_ref[1:2, :]
        dt = jnp.exp(s5v_ref[2:3, :])
        xr = lam_re * dt
        th = lam_im * dt

        def power(k):
            ek = jnp.exp(k * xr)
            return ek * jnp.cos(k * th), ek * jnp.sin(k * th)

        a_re, a_im = power(1.0)
        den = lam_re * lam_re + lam_im * lam_im
        coef_ref[0:1, :] = ((a_re - 1.0) * lam_re + a_im * lam_im) / den
        coef_ref[1:2, :] = (a_im * lam_re - (a_re - 1.0) * lam_im) / den
        are_ref[...] = jnp.broadcast_to(a_re, are_ref.shape)
        aim_ref[...] = jnp.broadcast_to(a_im, aim_ref.shape)
        al_re, al_im = power(float(n_pos))
        alre_ref[...] = jnp.broadcast_to(al_re, alre_ref.shape)
        alim_ref[...] = jnp.broadcast_to(al_im, alim_ref.shape)
        for j, sh in enumerate((1, 2, 4)):
            p_re, p_im = power(float(sh * n_pos))
            keep = row8 >= sh
            wre_ref[prow(j), :] = jnp.where(keep, p_re, 0.0)
            wim_ref[prow(j), :] = jnp.where(keep, p_im, 0.0)
        kk = (lax.broadcasted_iota(jnp.int32, (n_pos, S5_LANES), 0) + 1).astype(F32)
        t_re, t_im = power(kk)
        for pos in range(n_pos):
            ptre_ref[prow(pos), :] = jnp.broadcast_to(t_re[pos:pos + 1, :], (SUBLANES, S5_LANES))
            ptim_ref[prow(pos), :] = jnp.broadcast_to(t_im[pos:pos + 1, :], (SUBLANES, S5_LANES))

    def vrow(r):
        return vec_ref[r:r + 1, :]

    h = h_ref[...]
    xn = _rms(h, g1_ref[...])
    xnb = xn.astype(BF16)
    proj = _dot(xnb, win_ref[...])
    a_b = proj[:, 0:256]
    a_c = proj[:, 256:512]
    a_x = proj[:, 512:768]
    p_u = proj[:, 768:1024]
    s_u = proj[:, 1024:1280]
    l_x = proj[:, 1280:1536]
    l_g = proj[:, 1536:1792]

    def gate(k):
        return jax.nn.sigmoid(_dot(xnb, wg_ref[:, k * D_MODEL:(k + 1) * D_MODEL])
                              + bg_ref[:, k * D_MODEL:(k + 1) * D_MODEL])

    gates = [gate(0)]

    ext_a[HALO:HALO + ts, :] = a_c * a_x
    conv = ext_a[HALO - 2:HALO - 2 + ts, :] * vrow(_R_CA0)
    conv = conv + ext_a[HALO - 1:HALO - 1 + ts, :] * vrow(_R_CA1)
    conv = conv + ext_a[HALO:HALO + ts, :] * vrow(_R_CA2)
    ya = a_b * conv
    ext_a[0:HALO, :] = ext_a[ts:ts + HALO, :]

    ext_b[HALO:HALO + ts, :] = p_u
    lane = lax.broadcasted_iota(jnp.int32, (1, MIX_W), 1)
    grp = jnp.right_shift(lane, 6)
    win = p_u
    acc = p_u
    sh = 1
    for gi, w in enumerate(POOL_WINDOWS):
        while sh < w:
            acc = acc + ext_b[HALO - sh:HALO - sh + ts, :]
            sh += 1
        if gi > 0:
            win = jnp.where(grp >= gi, acc, win)
        else:
            win = acc
    wlane = jnp.where(grp == 0, 2.0, jnp.where(grp == 1, 4.0, jnp.where(grp == 2, 8.0, 16.0)))
    tpos = (s * ts + lax.broadcasted_iota(jnp.int32, (ts, MIX_W), 0) + 1).astype(F32)
    cnt = jnp.minimum(tpos, wlane)
    pooled = win / cnt - p_u
    yb = (_dot(pooled.astype(BF16), wpool_ref[...]) + vrow(_R_POOL_B)) * vrow(_R_POOL_SCALE)
    ext_b[0:HALO, :] = ext_b[ts:ts + HALO, :]

    put(su_ref, s_u)
    u_cm = to_chunk_major(su_ref)
    bu = _dot(u_cm.astype(BF16), bbig_ref[...])
    bre = bu[:, :S5_LANES]
    bim = bu[:, S5_LANES:]
    c_re = coef_ref[0:1, :]
    c_im = coef_ref[1:2, :]
    st_ref[:, :S5_LANES] = c_re * bre - c_im * bim
    st_ref[:, S5_LANES:] = c_re * bim + c_im * bre
    gates.append(gate(1))
    a_re = are_ref[...]
    a_im = aim_ref[...]
    hr = st_ref[prow(0), :S5_LANES]
    hi = st_ref[prow(0), S5_LANES:]
    for pos in range(1, n_pos):
        hr, hi = (a_re * hr - a_im * hi + st_ref[prow(pos), :S5_LANES],
                  a_re * hi + a_im * hr + st_ref[prow(pos), S5_LANES:])
        st_ref[prow(pos), :S5_LANES] = hr
        st_ref[prow(pos), S5_LANES:] = hi
    first = row8 == 0
    fr = jnp.where(first, s5c_ref[0:1, :S5_LANES], pltpu.roll(hr, 1, axis=0))
    fi = jnp.where(first, s5c_ref[0:1, S5_LANES:], pltpu.roll(hi, 1, axis=0))
    for jj, shift in enumerate((1, 2, 4)):
        wr = wre_ref[prow(jj), :]
        wi = wim_ref[prow(jj), :]
        sr = pltpu.roll(fr, shift, axis=0)
        si = pltpu.roll(fi, shift, axis=0)
        fr, fi = fr + (wr * sr - wi * si), fi + (wr * si + wi * sr)
    al_re = alre_ref[...]
    al_im = alim_ref[...]
    nxt_re = al_re * fr - al_im * fi + hr
    nxt_im = al_re * fi + al_im * fr + hi
    s5c_ref[0:1, :S5_LANES] = nxt_re[SUBLANES - 1:SUBLANES, :]
    s5c_ref[0:1, S5_LANES:] = nxt_im[SUBLANES - 1:SUBLANES, :]
    gates.append(gate(2))
    for pos in range(n_pos):
        pr = ptre_ref[prow(pos), :]
        pi = ptim_ref[prow(pos), :]
        st_ref[prow(pos), :S5_LANES] = st_ref[prow(pos), :S5_LANES] + (pr * fr - pi * fi)
        st_ref[prow(pos), S5_LANES:] = st_ref[prow(pos), S5_LANES:] + (pr * fi + pi * fr)
    yc = (_dot(st_ref[:, :S5_LANES].astype(BF16), cbig_ref[:S5_LANES, :])
          + _dot(st_ref[:, S5_LANES:].astype(BF16), cbig_ref[S5_LANES:, :]))
    yc = yc + vrow(_R_S5_D) * u_cm
    yc = jax.nn.gelu(yc)
    yc = yc * jax.nn.sigmoid(_dot(yc.astype(BF16), glu_ref[...]) + vrow(_R_GLU_B))
    for pos in range(n_pos):
        store_time_major(yc_ref, pos, yc[prow(pos), :])
    yc = get(yc_ref)

    gates.append(gate(3))

    ext_d[HALO:HALO + ts, :] = l_x
    xc = ext_d[HALO - 3:HALO - 3 + ts, :] * vrow(_R_CD0)
    xc = xc + ext_d[HALO - 2:HALO - 2 + ts, :] * vrow(_R_CD1)
    xc = xc + ext_d[HALO - 1:HALO - 1 + ts, :] * vrow(_R_CD2)
    xc = xc + ext_d[HALO:HALO + ts, :] * vrow(_R_CD3)
    xc = xc + vrow(_R_CONV_B)
    ext_d[0:HALO, :] = ext_d[ts:ts + HALO, :]
    put(xc_ref, xc)
    xc = to_chunk_major(xc_ref)
    xcb = xc.astype(BF16)
    r_gate = jax.nn.sigmoid(_dot(xcb, wa_ref[...]) + vrow(_R_BA))
    i_gate = jax.nn.sigmoid(_dot(xcb, wx_ref[...]) + vrow(_R_BX))
    z = -vrow(_R_LAM)
    softplus = jnp.maximum(z, 0.0) + jnp.log1p(jnp.exp(-jnp.abs(z)))
    log_a = -LRU_C * r_gate * softplus
    a_t = jnp.exp(log_a)
    mult = jnp.sqrt(1.0 - a_t * a_t)
    b_t = mult * (i_gate * xc)
    hh = b_t[prow(0), :]
    aa = a_t[prow(0), :]
    h_loc = [hh]
    a_cum = [aa]
    for pos in range(1, n_pos):
        a_pos = a_t[prow(pos), :]
        hh = a_pos * hh + b_t[prow(pos), :]
        aa = a_pos * aa
        h_loc.append(hh)
        a_cum.append(aa)
    row8w = lax.broadcasted_iota(jnp.int32, (SUBLANES, MIX_W), 0)
    f = jnp.where(row8w == 0, lruc_ref[0:1, :], pltpu.roll(hh, 1, axis=0))
    m = pltpu.roll(aa, 1, axis=0)
    for shift in (1, 2, 4):
        keep = row8w >= shift
        f_s = pltpu.roll(f, shift, axis=0)
        m_s = pltpu.roll(m, shift, axis=0)
        f = jnp.where(keep, m * f_s + f, f)
        m = jnp.where(keep, m * m_s, m)
    lruc_ref[0:1, :] = (aa * f + hh)[SUBLANES - 1:SUBLANES, :]
    for pos in range(n_pos):
        store_time_major(lb_ref, pos, h_loc[pos] + a_cum[pos] * f)
    yd = get(lb_ref) * jax.nn.gelu(l_g)

    merged = None
    for k, yk in enumerate((ya, yb, yc, yd)):
        term = gates[k] * _dot(yk.astype(BF16), bw_ref[k])
        merged = term if merged is None else merged + term
    o_ref[...] = h + _dot(merged.astype(BF16), wout_ref[...])


def _const_spec(shape):
    nd = len(shape)
    return pl.BlockSpec(shape, lambda b, s: (0,) * nd, pipeline_mode=pl.Buffered(1))


def _mixer(h, p, batch, seq):
    ts = SEQ_TILE
    ns = seq // ts
    t = batch * seq
    weights = (p["g1"], p["w_in"], p["vec"], p["s5v"], p["wpool"], p["bbig"], p["cbig"], p["glu_w"],
               p["wa"], p["wx"], p["wg"], p["bg"], p["bw"], p["w_out"])
    in_specs = [pl.BlockSpec((ts, D_MODEL), lambda b, s: (b * ns + s, 0))]
    in_specs += [_const_spec(w.shape) for w in weights]
    return pl.pallas_call(
        _mixer_kernel,
        grid=(batch, ns),
        in_specs=in_specs,
        out_specs=pl.BlockSpec((ts, D_MODEL), lambda b, s: (b * ns + s, 0)),
        out_shape=jax.ShapeDtypeStruct((t, D_MODEL), F32),
        scratch_shapes=[
            pltpu.VMEM((HALO + ts, MIX_W), F32),
            pltpu.VMEM((HALO + ts, MIX_W), F32),
            pltpu.VMEM((HALO + ts, MIX_W), F32),
            pltpu.VMEM((ts, 2 * S5_LANES), F32),
            pltpu.VMEM((2, ts, LANES), F32),
            pltpu.VMEM((2, ts, LANES), F32),
            pltpu.VMEM((2, ts, LANES), F32),
            pltpu.VMEM((2, ts, LANES), F32),
            pltpu.VMEM((SUBLANES, 2 * S5_LANES), F32),
            pltpu.VMEM((SUBLANES, MIX_W), F32),
            pltpu.VMEM((SUBLANES, S5_LANES), F32),
            pltpu.VMEM((SUBLANES, S5_LANES), F32),
            pltpu.VMEM((SUBLANES, S5_LANES), F32),
            pltpu.VMEM((SUBLANES, S5_LANES), F32),
            pltpu.VMEM((3 * SUBLANES, S5_LANES), F32),
            pltpu.VMEM((3 * SUBLANES, S5_LANES), F32),
            pltpu.VMEM((ts, S5_LANES), F32),
            pltpu.VMEM((ts, S5_LANES), F32),
            pltpu.VMEM((SUBLANES, S5_LANES), F32),
        ],
        compiler_params=pltpu.CompilerParams(
            dimension_semantics=("arbitrary", "arbitrary"),
            vmem_limit_bytes=VMEM_LIMIT),
        name="mixer",
    )(h, *weights)


def _router_kernel(h_ref, g2_ref, wr_ref, br_ref, xn_ref, ri_ref, wcol_ref, cnt_ref, carry_ref):
    tm = h_ref.shape[0]
    i = pl.program_id(0)

    @pl.when(i == 0)
    def _():
        carry_ref[...] = jnp.zeros(carry_ref.shape, F32)

    xn = _rms(h_ref[...], g2_ref[...])
    xn_ref[...] = _pack_rows(xn)
    logits = lax.dot_general(wr_ref[...], xn.astype(BF16), (((1,), (1,)), ((), ())),
                             preferred_element_type=F32) + br_ref[...]
    row8 = lax.broadcasted_iota(jnp.int32, (SUBLANES, tm), 0)
    neg_inf = jnp.float32(-jnp.inf)
    gl = jnp.where(row8 < N_GROUPS, logits[0:SUBLANES, :], neg_inf)
    gmax = jnp.max(gl, axis=0, keepdims=True)
    ge = jnp.exp(gl - gmax)
    gp = ge / jnp.sum(ge, axis=0, keepdims=True)
    g_val = jnp.max(gp, axis=0, keepdims=True)
    g_idx = jnp.min(jnp.where(gp == g_val, row8, SUBLANES), axis=0, keepdims=True)
    sel = logits[4 * SUBLANES:5 * SUBLANES, :]
    for g in (2, 1, 0):
        sel = jnp.where(g_idx == g, logits[(g + 1) * SUBLANES:(g + 2) * SUBLANES, :], sel)
    v1 = jnp.max(sel, axis=0, keepdims=True)
    i1 = jnp.min(jnp.where(sel == v1, row8, SUBLANES), axis=0, keepdims=True)
    sel2 = jnp.where(row8 == i1, neg_inf, sel)
    v2 = jnp.max(sel2, axis=0, keepdims=True)
    i2 = jnp.min(jnp.where(sel2 == v2, row8, SUBLANES), axis=0, keepdims=True)
    e2 = jnp.exp(v2 - v1)
    denom = 1.0 + e2
    w1 = (1.0 / denom) * g_val
    w2 = (e2 / denom) * g_val
    eid0 = g_idx * EXP_PER_GROUP + i1
    eid1 = g_idx * EXP_PER_GROUP + i2
    e32 = lax.broadcasted_iota(jnp.int32, (N_EXPERTS, tm), 0)
    oh0 = (e32 == eid0).astype(F32)
    oh1 = (e32 == eid1).astype(F32)
    oh = oh0 + oh1
    before = (lax.broadcasted_iota(jnp.int32, (tm, tm), 0)
              < lax.broadcasted_iota(jnp.int32, (tm, tm), 1)).astype(BF16)
    base = _dot(oh.astype(BF16), before) + carry_ref[:, 0:1]
    rank0 = jnp.sum(oh0 * base, axis=0, keepdims=True)
    rank1 = jnp.sum(oh1 * base, axis=0, keepdims=True)
    new_carry = carry_ref[...] + jnp.sum(oh, axis=1, keepdims=True)
    carry_ref[...] = new_carry
    cnt_ref[...] = new_carry.astype(jnp.int32)
    ri_ref[...] = jnp.zeros(ri_ref.shape, jnp.int32)
    ri_ref[0:1, :] = eid0
    ri_ref[1:2, :] = eid1
    ri_ref[2:3, :] = rank0.astype(jnp.int32)
    ri_ref[3:4, :] = rank1.astype(jnp.int32)
    rows = lax.broadcasted_iota(jnp.int32, (ROUTER_ROWS, tm), 0)
    wrows = jnp.where(rows == 0, w1, jnp.where(rows == 1, w2, 0.0))
    wcol_ref[...] = wrows.T


def _router(h, g2, wr_t, br_col):
    t = h.shape[0]
    tm = ROUTE_TILE
    return pl.pallas_call(
        _router_kernel,
        grid=(t // tm,),
        in_specs=[
            pl.BlockSpec((tm, D_MODEL), lambda i: (i, 0)),
            pl.BlockSpec((1, D_MODEL), lambda i: (0, 0)),
            pl.BlockSpec((ROUTER_ROWS, D_MODEL), lambda i: (0, 0)),
            pl.BlockSpec((ROUTER_ROWS, 1), lambda i: (0, 0)),
        ],
        out_specs=[
            pl.BlockSpec((tm, ROW_WORDS), lambda i: (i, 0)),
            pl.BlockSpec((SUBLANES, tm), lambda i: (0, i)),
            pl.BlockSpec((tm, ROUTER_ROWS), lambda i: (i, 0)),
            pl.BlockSpec((N_EXPERTS, 128), lambda i: (0, 0)),
        ],
        out_shape=[
            jax.ShapeDtypeStruct((t, ROW_WORDS), jnp.uint32),
            jax.ShapeDtypeStruct((SUBLANES, t), jnp.int32),
            jax.ShapeDtypeStruct((t, ROUTER_ROWS), F32),
            jax.ShapeDtypeStruct((N_EXPERTS, 128), jnp.int32),
        ],
        scratch_shapes=[pltpu.VMEM((N_EXPERTS, 128), F32)],
        compiler_params=pltpu.CompilerParams(
            dimension_semantics=("arbitrary",), vmem_limit_bytes=VMEM_LIMIT),
        name="router",
    )(h, g2, wr_t, br_col)


def _dest_kernel(ri_ref, ps_ref, o_ref):
    tm = ri_ref.shape[1]
    e32 = lax.broadcasted_iota(jnp.int32, (N_EXPERTS, tm), 0)
    o_ref[...] = jnp.zeros(o_ref.shape, jnp.int32)
    for k in range(2):
        start = jnp.sum(jnp.where(e32 == ri_ref[k:k + 1, :], ps_ref[...], 0.0), axis=0, keepdims=True)
        o_ref[k:k + 1, :] = start.astype(jnp.int32) + ri_ref[2 + k:3 + k, :]


def _dest(route_i, pad_start_col):
    t = route_i.shape[1]
    tm = DEST_TILE
    return pl.pallas_call(
        _dest_kernel,
        grid=(t // tm,),
        in_specs=[
            pl.BlockSpec((SUBLANES, tm), lambda i: (0, i)),
            pl.BlockSpec((N_EXPERTS, 1), lambda i: (0, 0)),
        ],
        out_specs=pl.BlockSpec((SUBLANES, tm), lambda i: (0, i)),
        out_shape=jax.ShapeDtypeStruct((SUBLANES, t), jnp.int32),
        compiler_params=pltpu.CompilerParams(dimension_semantics=("arbitrary",)),
        name="dest",
    )(route_i, pad_start_col)


def _dispatch_kernel(dest_ref, fs_ref, fl_ref, meta_ref, x_ref, xs_ref, zbuf, sem, zsem):
    tm = x_ref.shape[0]
    bm = zbuf.shape[0]
    n_blocks = xs_ref.shape[0] // bm
    i = pl.program_id(0)
    n_tok = pl.num_programs(0) * tm
    base = i * tm

    def pad_copy(off, p):
        return pltpu.make_async_copy(zbuf.at[pl.ds(0, p), :], xs_ref.at[pl.ds(off, p), :], zsem)

    def for_each_fill(act):
        def segment(e, _):
            off = fs_ref[e]
            n = fl_ref[e]
            head = n & (SUBLANES - 1)
            for j in range(SUBLANES - 1):
                @pl.when(j < head)
                def _(off=off, j=j):
                    act(pad_copy(off + j, 1))

            off = pl.multiple_of(off + head, SUBLANES)
            for p in PAD_PIECES:
                if p < SUBLANES:
                    continue
                piece = n & p

                @pl.when(piece != 0)
                def _(off=off, p=p):
                    act(pad_copy(off, p))

                off = pl.multiple_of(off + piece, SUBLANES)
            return 0

        lax.fori_loop(0, N_EXPERTS, segment, 0)

        def tail(b, _):
            act(pad_copy(pl.multiple_of(b * bm, bm), bm))
            return 0

        lax.fori_loop(meta_ref[0], n_blocks, tail, 0)

    @pl.when(i == 0)
    def _():
        zbuf[...] = jnp.zeros(zbuf.shape, jnp.uint32)
        for_each_fill(lambda c: c.start())
        for_each_fill(lambda c: c.wait())

    def row_copy(r, d):
        return pltpu.make_async_copy(x_ref.at[pl.ds(r, 1), :], xs_ref.at[pl.ds(d, 1), :], sem)

    for r in range(tm):
        row_copy(r, dest_ref[base + r]).start(priority=0)
        row_copy(r, dest_ref[n_tok + base + r]).start(priority=1)

    for _ in range(2):
        pltpu.make_async_copy(x_ref, xs_ref.at[pl.ds(0, tm), :], sem).wait()


def _dispatch(dest_flat, fill_start, fill_len, meta, xn, n_rows):
    t = xn.shape[0]
    tm = DISPATCH_TILE
    return pl.pallas_call(
        _dispatch_kernel,
        grid_spec=pltpu.PrefetchScalarGridSpec(
            num_scalar_prefetch=4,
            grid=(t // tm,),
            in_specs=[pl.BlockSpec((tm, ROW_WORDS), lambda i, *_: (i, 0))],
            out_specs=pl.BlockSpec(memory_space=pl.ANY),
            scratch_shapes=[
                pltpu.VMEM((EXPERT_BLOCK, ROW_WORDS), jnp.uint32),
                pltpu.SemaphoreType.DMA(()),
                pltpu.SemaphoreType.DMA(()),
            ],
        ),
        out_shape=jax.ShapeDtypeStruct((n_rows, ROW_WORDS), jnp.uint32),
        compiler_params=pltpu.CompilerParams(dimension_semantics=("arbitrary",)),
        name="dispatch",
    )(dest_flat, fill_start, fill_len, meta, xn)


def _expert_kernel(ss_ref, nc_ref, meta_ref, xs_ref, w1_ref, w3_ref, w2_ref, y_ref,
                   xbuf, ybuf, w1b, w3b, w2b, insem, outsem):
    e = pl.program_id(0)
    ch = xbuf.shape[1]
    n_blocks = y_ref.shape[0] // ch
    start = ss_ref[e]
    nchunk = nc_ref[e]

    n_in = xbuf.shape[0]
    n_exp = pl.num_programs(0)

    def rows(seg_start, c):
        return pl.ds(pl.multiple_of(seg_start + c * ch, ch), ch)

    def in_copy(seg_start, c, slot):
        return pltpu.make_async_copy(xs_ref.at[rows(seg_start, c), :], xbuf.at[slot], insem.at[slot])

    def out_copy(c, slot):
        return pltpu.make_async_copy(ybuf.at[slot], y_ref.at[rows(start, c), :], outsem.at[slot])

    def prime(seg_start, seg_chunks):
        for k in range(n_in):
            @pl.when(k < seg_chunks)
            def _(k=k):
                in_copy(seg_start, k, k).start(priority=CHUNK_DMA_PRIORITY)

    @pl.when(e == 0)
    def _():
        prime(start, nchunk)

    w1b[...] = w1_ref[0, 0].astype(BF16)
    w3b[...] = w3_ref[0, 0].astype(BF16)
    w2b[...] = w2_ref[0, 0].astype(BF16)

    def body(c, _):
        slot = c % 2
        islot = c % n_in
        in_copy(start, c, islot).wait()

        @pl.when(c >= 2)
        def _():
            out_copy(c - 2, slot).wait()

        lo, hi = _unpack_rows(xbuf[islot])
        xb = jnp.concatenate([lo.astype(BF16), hi.astype(BF16)], axis=1)
        hid = jax.nn.silu(_dot(xb, w1b[...])) * _dot(xb, w3b[...])
        ybuf[slot] = _pack_rows(_dot(hid.astype(BF16), w2b[...]))
        out_copy(c, slot).start(priority=CHUNK_DMA_PRIORITY)

        @pl.when(c + n_in < nchunk)
        def _():
            in_copy(start, c + n_in, islot).start(priority=CHUNK_DMA_PRIORITY)

        return 0

    lax.fori_loop(0, nchunk, body, 0)

    @pl.when(e + 1 < n_exp)
    def _():
        nxt = jnp.minimum(e + 1, n_exp - 1)
        prime(ss_ref[nxt], nc_ref[nxt])

    @pl.when(nchunk >= 2)
    def _():
        out_copy(nchunk - 2, nchunk % 2).wait()

    @pl.when(nchunk >= 1)
    def _():
        out_copy(nchunk - 1, (nchunk - 1) % 2).wait()

    @pl.when(e == pl.num_programs(0) - 1)
    def _():
        ybuf[0] = jnp.zeros(ybuf.shape[1:], jnp.uint32)

        def tail_copy(b):
            return pltpu.make_async_copy(
                ybuf.at[0], y_ref.at[pl.ds(pl.multiple_of(b * ch, ch), ch), :], outsem.at[0])

        def tail_start(b, _):
            tail_copy(b).start()
            return 0

        def tail_wait(b, _):
            tail_copy(b).wait()
            return 0

        lax.fori_loop(meta_ref[0], n_blocks, tail_start, 0)
        lax.fori_loop(meta_ref[0], n_blocks, tail_wait, 0)


def _experts(layer, seg_start, seg_chunks, meta, xs, w1, w3, w2):
    n_rows = xs.shape[0]
    ch = EXPERT_BLOCK

    def w_map(e, *_):
        return (layer, e, 0, 0)

    return pl.pallas_call(
        _expert_kernel,
        grid_spec=pltpu.PrefetchScalarGridSpec(
            num_scalar_prefetch=3,
            grid=(N_EXPERTS,),
            in_specs=[
                pl.BlockSpec(memory_space=pl.ANY),
                pl.BlockSpec((1, 1, D_MODEL, D_EXPERT), w_map),
                pl.BlockSpec((1, 1, D_MODEL, D_EXPERT), w_map),
                pl.BlockSpec((1, 1, D_EXPERT, D_MODEL), w_map),
            ],
            out_specs=pl.BlockSpec(memory_space=pl.ANY),
            scratch_shapes=[
                pltpu.VMEM((EXPERT_IN_BUFFERS, ch, ROW_WORDS), jnp.uint32),
                pltpu.VMEM((2, ch, ROW_WORDS), jnp.uint32),
                pltpu.VMEM((D_MODEL, D_EXPERT), BF16),
                pltpu.VMEM((D_MODEL, D_EXPERT), BF16),
                pltpu.VMEM((D_EXPERT, D_MODEL), BF16),
                pltpu.SemaphoreType.DMA((EXPERT_IN_BUFFERS,)),
                pltpu.SemaphoreType.DMA((2,)),
            ],
        ),
        out_shape=jax.ShapeDtypeStruct((n_rows, ROW_WORDS), jnp.uint32),
        compiler_params=pltpu.CompilerParams(
            dimension_semantics=("arbitrary",), vmem_limit_bytes=VMEM_LIMIT),
        name="experts",
    )(seg_start, seg_chunks, meta, xs, w1, w3, w2)


def _combine_kernel(dest_ref, h_ref, wcol_ref, g_ref, y_ref, o_ref, buf, sem, *, final_norm):
    tc = h_ref.shape[0]
    i = pl.program_id(0)
    n = pl.num_programs(0)

    def row_copy(p, slot, k, r):
        return pltpu.make_async_copy(y_ref.at[pl.ds(p, 1), :],
                                     buf.at[2 * slot + k, pl.ds(r, 1), :], sem.at[slot])

    n_tok = n * tc

    def issue(tile, slot):
        tok0 = tile * tc
        for r in range(tc):
            row_copy(dest_ref[tok0 + r], slot, 0, r).start(priority=0)
            row_copy(dest_ref[n_tok + tok0 + r], slot, 1, r).start(priority=1)

    @pl.when(i == 0)
    def _():
        issue(0, 0)

    @pl.when(i + 1 < n)
    def _():
        issue(i + 1, (i + 1) % 2)

    slot = i % 2

    for k in range(2):
        pltpu.make_async_copy(y_ref.at[pl.ds(0, tc), :], buf.at[2 * slot + k], sem.at[slot]).wait()
    w = wcol_ref[...]
    lo0, hi0 = _unpack_rows(buf[2 * slot])
    lo1, hi1 = _unpack_rows(buf[2 * slot + 1])
    moe = jnp.concatenate([w[:, 0:1] * lo0 + w[:, 1:2] * lo1,
                           w[:, 0:1] * hi0 + w[:, 1:2] * hi1], axis=1)
    out = h_ref[...] + moe
    if final_norm:
        out = _rms(out, g_ref[...])
    o_ref[...] = out


def _combine(dest_flat, h, wcol, g, y, final_norm):
    t = h.shape[0]
    tc = COMBINE_TILE
    return pl.pallas_call(
        functools.partial(_combine_kernel, final_norm=final_norm),
        grid_spec=pltpu.PrefetchScalarGridSpec(
            num_scalar_prefetch=1,
            grid=(t // tc,),
            in_specs=[
                pl.BlockSpec((tc, D_MODEL), lambda i, d: (i, 0)),
                pl.BlockSpec((tc, ROUTER_ROWS), lambda i, d: (i, 0)),
                pl.BlockSpec((1, D_MODEL), lambda i, d: (0, 0)),
                pl.BlockSpec(memory_space=pl.ANY),
            ],
            out_specs=pl.BlockSpec((tc, D_MODEL), lambda i, d: (i, 0)),
            scratch_shapes=[
                pltpu.VMEM((4, tc, ROW_WORDS), jnp.uint32),
                pltpu.SemaphoreType.DMA((2,)),
            ],
        ),
        out_shape=jax.ShapeDtypeStruct((t, D_MODEL), F32),
        compiler_params=pltpu.CompilerParams(
            dimension_semantics=("arbitrary",), vmem_limit_bytes=VMEM_LIMIT),
        name="combine",
    )(dest_flat, h, wcol, g, y)


def _block_diag(w):
    g, i, o = w.shape
    eye = jnp.eye(g, dtype=w.dtype)
    return jnp.einsum("gio,gk->giko", w, eye).reshape(g * i, g * o)


def _layer_params(l, a):
    row = lambda v: v.reshape(1, -1)
    vec_rows = [a["pool_b"][l].reshape(-1), a["pool_scale"][l], a["s5_d"][l], a["s5_glu_b"][l],
                a["lru_conv_b"][l], a["lru_ba"][l].reshape(-1), a["lru_bx"][l].reshape(-1),
                a["lru_lambda"][l],
                a["conv_a_w"][l][0], a["conv_a_w"][l][1], a["conv_a_w"][l][2],
                a["lru_conv_w"][l][0], a["lru_conv_w"][l][1], a["lru_conv_w"][l][2],
                a["lru_conv_w"][l][3], jnp.zeros((MIX_W,), F32)]
    s5v = jnp.stack([a["s5_lambda_re"][l].reshape(-1), a["s5_lambda_im"][l].reshape(-1),
                     jnp.repeat(a["s5_log_step"][l], S5_STATE)]
                    + [jnp.zeros((S5_LANES,), F32)] * 5)
    b_re = _block_diag(jnp.swapaxes(a["s5_b_re"][l], 1, 2))
    b_im = _block_diag(jnp.swapaxes(a["s5_b_im"][l], 1, 2))
    c_re = _block_diag(jnp.swapaxes(a["s5_c_re"][l], 1, 2))
    c_im = _block_diag(jnp.swapaxes(a["s5_c_im"][l], 1, 2))
    wr_t = jnp.zeros((ROUTER_ROWS, D_MODEL), F32)
    wr_t = wr_t.at[0:N_GROUPS].set(a["router_group_w"][l].T)
    wr_t = wr_t.at[SUBLANES:SUBLANES + N_EXPERTS].set(a["router_expert_w"][l].T)
    br = jnp.zeros((ROUTER_ROWS,), F32)
    br = br.at[0:N_GROUPS].set(a["router_group_b"][l])
    br = br.at[SUBLANES:SUBLANES + N_EXPERTS].set(a["router_expert_b"][l])
    return {
        "g1": row(a["norm1_g"][l]),
        "w_in": a["w_in"][l].astype(BF16),
        "vec": jnp.stack(vec_rows),
        "s5v": s5v,
        "wpool": _block_diag(a["pool_w"][l]).astype(BF16),
        "bbig": jnp.concatenate([b_re, b_im], axis=1).astype(BF16),
        "cbig": jnp.concatenate([c_re, -c_im], axis=0).astype(BF16),
        "glu_w": a["s5_glu_w"][l].astype(BF16),
        "wa": _block_diag(a["lru_wa"][l]).astype(BF16),
        "wx": _block_diag(a["lru_wx"][l]).astype(BF16),
        "wg": a["merge_gate_w"][l].astype(BF16),
        "bg": row(a["merge_gate_b"][l]),
        "bw": a["branch_w"][l].astype(BF16),
        "w_out": a["w_out"][l].astype(BF16),
        "g2": row(a["norm2_g"][l]),
        "wr_t": wr_t.astype(BF16),
        "br": br.reshape(ROUTER_ROWS, 1),
    }


def _moe(layer, h, p, w1, w3, w2, g_final, final_norm):
    t = h.shape[0]
    bm = EXPERT_BLOCK
    xn, route_i, wcol, counts = _router(h, p["g2"], p["wr_t"], p["br"])
    cnt = counts[:, 0]
    padded = ((cnt + bm - 1) // bm) * bm
    pad_end = jnp.cumsum(padded)
    pad_start = pad_end - padded
    n_rows = (-(-(2 * t) // bm)) * bm + N_EXPERTS * bm
    n_blocks = n_rows // bm
    meta = (pad_end[-1:] // bm).astype(jnp.int32)
    dest = _dest(route_i, pad_start.astype(F32).reshape(N_EXPERTS, 1))
    dest_flat = dest[0:2].reshape(-1)
    xs = _dispatch(dest_flat, (pad_start + cnt).astype(jnp.int32), (padded - cnt).astype(jnp.int32),
                   meta, xn, n_rows)
    y = _experts(layer, pad_start.astype(jnp.int32), (padded // bm).astype(jnp.int32), meta,
                 xs, w1, w3, w2)
    return _combine(dest_flat, h, wcol, g_final, y, final_norm)


def kernel(x, norm1_g, w_in, conv_a_w, pool_w, pool_b, pool_scale, s5_lambda_re, s5_lambda_im, s5_log_step, s5_b_re, s5_b_im, s5_c_re, s5_c_im, s5_d, s5_glu_w, s5_glu_b, lru_conv_w, lru_conv_b, lru_wa, lru_ba, lru_wx, lru_bx, lru_lambda, merge_gate_w, merge_gate_b, branch_w, w_out, norm2_g, router_group_w, router_group_b, router_expert_w, router_expert_b, expert_w1, expert_w3, expert_w2, final_norm_g):
    a = dict(norm1_g=norm1_g, w_in=w_in, conv_a_w=conv_a_w, pool_w=pool_w, pool_b=pool_b,
             pool_scale=pool_scale, s5_lambda_re=s5_lambda_re, s5_lambda_im=s5_lambda_im,
             s5_log_step=s5_log_step, s5_b_re=s5_b_re, s5_b_im=s5_b_im, s5_c_re=s5_c_re,
             s5_c_im=s5_c_im, s5_d=s5_d, s5_glu_w=s5_glu_w, s5_glu_b=s5_glu_b,
             lru_conv_w=lru_conv_w, lru_conv_b=lru_conv_b, lru_wa=lru_wa, lru_ba=lru_ba,
             lru_wx=lru_wx, lru_bx=lru_bx, lru_lambda=lru_lambda, merge_gate_w=merge_gate_w,
             merge_gate_b=merge_gate_b, branch_w=branch_w, w_out=w_out, norm2_g=norm2_g,
             router_group_w=router_group_w, router_group_b=router_group_b,
             router_expert_w=router_expert_w, router_expert_b=router_expert_b)
    batch, seq, d = x.shape
    depth = norm1_g.shape[0]
    h = x.reshape(batch * seq, d)
    g_final = final_norm_g.reshape(1, d)
    for l in range(depth):
        p = _layer_params(l, a)
        h = _mixer(h, p, batch, seq)
        h = _moe(l, h, p, expert_w1, expert_w3, expert_w2, g_final, l == depth - 1)
    return h.reshape(batch, seq, d)
```

```python
import functools

import jax
import jax.numpy as jnp
from jax import lax
from jax.experimental import pallas as pl
from jax.experimental.pallas import tpu as pltpu

F32 = jnp.float32
BF16 = jnp.bfloat16

D_MODEL = 1024
MIX_W = 256
N_BRANCH = 4
POOL_WINDOWS = (2, 4, 8, 16)
POOL_GC = 64
S5_GROUPS = 16
S5_GROUP_CH = 16
S5_STATE = 64
S5_LANES = S5_GROUPS * S5_STATE
S5_EIG_CLIP = -1e-4
LRU_HEADS = 4
LRU_C = 8.0
IN_COLS = 1792
N_GROUPS = 4
EXP_PER_GROUP = 8
N_EXPERTS = 32
D_EXPERT = 512
EPS = 1e-6

SUBLANES = 8
LANES = 128
ROW_WORDS = D_MODEL // 2
HALO = 16
SEQ_TILE = 512
ROUTE_TILE = 512
DISPATCH_TILE = 512
COMBINE_TILE = 512
EXPERT_BLOCK = 256
ROUTER_ROWS = 128
DEST_TILE = 2048
EXPERT_IN_BUFFERS = 3
CHUNK_DMA_PRIORITY = 1
PAD_PIECES = tuple(EXPERT_BLOCK >> (k + 1) for k in range(EXPERT_BLOCK.bit_length() - 1))
VMEM_LIMIT = 56 * 1024 * 1024

(_R_POOL_B, _R_POOL_SCALE, _R_S5_D, _R_GLU_B, _R_CONV_B, _R_BA, _R_BX, _R_LAM,
 _R_CA0, _R_CA1, _R_CA2, _R_CD0, _R_CD1, _R_CD2, _R_CD3) = range(15)


def _rms(x, g):
    return x * lax.rsqrt(jnp.mean(x * x, axis=-1, keepdims=True) + EPS) * g


def _dot(a, b):
    return jnp.dot(a, b, preferred_element_type=F32)


def _pack_rows(x):
    half = x.shape[1] // 2
    lo = lax.bitcast_convert_type(x[:, :half].astype(BF16).astype(F32), jnp.uint32)
    hi = lax.bitcast_convert_type(x[:, half:].astype(BF16).astype(F32), jnp.uint32)
    return (hi & jnp.uint32(0xFFFF0000)) | (lo >> 16)


def _unpack_rows(w):
    lo = lax.bitcast_convert_type(w << 16, F32)
    hi = lax.bitcast_convert_type(w & jnp.uint32(0xFFFF0000), F32)
    return lo, hi


def _mixer_kernel(h_ref, g1_ref, win_ref, vec_ref, s5v_ref, wpool_ref, bbig_ref, cbig_ref,
                  glu_ref, wa_ref, wx_ref, wg_ref, bg_ref, bw_ref, wout_ref,
                  o_ref,
                  ext_a, ext_b, ext_d, st_ref, su_ref, xc_ref, yc_ref, lb_ref, s5c_ref, lruc_ref,
                  are_ref, aim_ref, alre_ref, alim_ref, wre_ref, wim_ref, ptre_ref, ptim_ref,
                  coef_ref):
    ts = h_ref.shape[0]
    n_pos = ts // SUBLANES
    s = pl.program_id(1)
    row8 = lax.broadcasted_iota(jnp.int32, (SUBLANES, S5_LANES), 0)

    def put(ref, val):
        for half in range(2):
            ref[half] = val[:, half * LANES:(half + 1) * LANES]

    def get(ref):
        return jnp.concatenate([ref[0], ref[1]], axis=1)

    def to_chunk_major(ref):
        return jnp.concatenate(
            [jnp.concatenate([ref[half, pl.ds(pos, SUBLANES, stride=n_pos), :] for pos in range(n_pos)],
                             axis=0) for half in range(2)], axis=1)

    def store_time_major(ref, pos, val):
        for half in range(2):
            ref[half, pl.ds(pos, SUBLANES, stride=n_pos), :] = val[:, half * LANES:(half + 1) * LANES]

    def prow(pos):
        return slice(pos * SUBLANES, (pos + 1) * SUBLANES)

    @pl.when(s == 0)
    def _start_of_sequence():
        zeros_halo = jnp.zeros((HALO, MIX_W), F32)
        ext_a[0:HALO, :] = zeros_halo
        ext_b[0:HALO, :] = zeros_halo
        ext_d[0:HALO, :] = zeros_halo
        s5c_ref[...] = jnp.zeros(s5c_ref.shape, F32)
        lruc_ref[...] = jnp.zeros(lruc_ref.shape, F32)

    @pl.when((s == 0) & (pl.program_id(0) == 0))
    def _s5_discretisation():
        lam_re = jnp.minimum(s5v_ref[0:1, :], S5_EIG_CLIP)
        lam_im = s5v_ref[1:2, :]
        dt = jnp.exp(s5v_ref[2:3, :])
        xr = lam_re * dt
        th = lam_im * dt

        def power(k):
            ek = jnp.exp(k * xr)
            return ek * jnp.cos(k * th), ek * jnp.sin(k * th)

        a_re, a_im = power(1.0)
        den = lam_re * lam_re + lam_im * lam_im
        coef_ref[0:1, :] = ((a_re - 1.0) * lam_re + a_im * lam_im) / den
        coef_ref[1:2, :] = (a_im * lam_re - (a_re - 1.0) * lam_im) / den
        are_ref[...] = jnp.broadcast_to(a_re, are_ref.shape)
        aim_ref[...] = jnp.broadcast_to(a_im, aim_ref.shape)
        al_re, al_im = power(float(n_pos))
        alre_ref[...] = jnp.broadcast_to(al_re, alre_ref.shape)
        alim_ref[...] = jnp.broadcast_to(al_im, alim_ref.shape)
        for j, sh in enumerate((1, 2, 4)):
            p_re, p_im = power(float(sh * n_pos))
            keep = row8 >= sh
            wre_ref[prow(j), :] = jnp.where(keep, p_re, 0.0)
            wim_ref[prow(j), :] = jnp.where(keep, p_im, 0.0)
        kk = (lax.broadcasted_iota(jnp.int32, (n_pos, S5_LANES), 0) + 1).astype(F32)
        t_re, t_im = power(kk)
        for pos in range(n_pos):
            ptre_ref[prow(pos), :] = jnp.broadcast_to(t_re[pos:pos + 1, :], (SUBLANES, S5_LANES))
            ptim_ref[prow(pos), :] = jnp.broadcast_to(t_im[pos:pos + 1, :], (SUBLANES, S5_LANES))

    def vrow(r):
        return vec_ref[r:r + 1, :]

    h = h_ref[...]
    xn = _rms(h, g1_ref[...])
    xnb = xn.astype(BF16)
    proj = _dot(xnb, win_ref[...])
    a_b = proj[:, 0:256]
    a_c = proj[:, 256:512]
    a_x = proj[:, 512:768]
    p_u = proj[:, 768:1024]
    s_u = proj[:, 1024:1280]
    l_x = proj[:, 1280:1536]
    l_g = proj[:, 1536:1792]

    def gate(k):
        return jax.nn.sigmoid(_dot(xnb, wg_ref[:, k * D_MODEL:(k + 1) * D_MODEL])
                              + bg_ref[:, k * D_MODEL:(k + 1) * D_MODEL])

    gates = [gate(0)]

    ext_a[HALO:HALO + ts, :] = a_c * a_x
    conv = ext_a[HALO - 2:HALO - 2 + ts, :] * vrow(_R_CA0)
    conv = conv + ext_a[HALO - 1:HALO - 1 + ts, :] * vrow(_R_CA1)
    conv = conv + ext_a[HALO:HALO + ts, :] * vrow(_R_CA2)
    ya = a_b * conv
    ext_a[0:HALO, :] = ext_a[ts:ts + HALO, :]

    ext_b[HALO:HALO + ts, :] = p_u
    lane = lax.broadcasted_iota(jnp.int32, (1, MIX_W), 1)
    grp = jnp.right_shift(lane, 6)
    win = p_u
    acc = p_u
    sh = 1
    for gi, w in enumerate(POOL_WINDOWS):
        while sh < w:
            acc = acc + ext_b[HALO - sh:HALO - sh + ts, :]
            sh += 1
        if gi > 0:
            win = jnp.where(grp >= gi, acc, win)
        else:
            win = acc
    wlane = jnp.where(grp == 0, 2.0, jnp.where(grp == 1, 4.0, jnp.where(grp == 2, 8.0, 16.0)))
    tpos = (s * ts + lax.broadcasted_iota(jnp.int32, (ts, MIX_W), 0) + 1).astype(F32)
    cnt = jnp.minimum(tpos, wlane)
    pooled = win / cnt - p_u
    yb = (_dot(pooled.astype(BF16), wpool_ref[...]) + vrow(_R_POOL_B)) * vrow(_R_POOL_SCALE)
    ext_b[0:HALO, :] = ext_b[ts:ts + HALO, :]

    put(su_ref, s_u)
    u_cm = to_chunk_major(su_ref)
    bu = _dot(u_cm.astype(BF16), bbig_ref[...])
    bre = bu[:, :S5_LANES]
    bim = bu[:, S5_LANES:]
    c_re = coef_ref[0:1, :]
    c_im = coef_ref[1:2, :]
    st_ref[:, :S5_LANES] = c_re * bre - c_im * bim
    st_ref[:, S5_LANES:] = c_re * bim + c_im * bre
    gates.append(gate(1))
    a_re = are_ref[...]
    a_im = aim_ref[...]
    hr = st_ref[prow(0), :S5_LANES]
    hi = st_ref[prow(0), S5_LANES:]
    for pos in range(1, n_pos):
        hr, hi = (a_re * hr - a_im * hi + st_ref[prow(pos), :S5_LANES],
                  a_re * hi + a_im * hr + st_ref[prow(pos), S5_LANES:])
        st_ref[prow(pos), :S5_LANES] = hr
        st_ref[prow(pos), S5_LANES:] = hi
    first = row8 == 0
    fr = jnp.where(first, s5c_ref[0:1, :S5_LANES], pltpu.roll(hr, 1, axis=0))
    fi = jnp.where(first, s5c_ref[0:1, S5_LANES:], pltpu.roll(hi, 1, axis=0))
    for jj, shift in enumerate((1, 2, 4)):
        wr = wre_ref[prow(jj), :]
        wi = wim_ref[prow(jj), :]
        sr = pltpu.roll(fr, shift, axis=0)
        si = pltpu.roll(fi, shift, axis=0)
        fr, fi = fr + (wr * sr - wi * si), fi + (wr * si + wi * sr)
    al_re = alre_ref[...]
    al_im = alim_ref[...]
    nxt_re = al_re * fr - al_im * fi + hr
    nxt_im = al_re * fi + al_im * fr + hi
    s5c_ref[0:1, :S5_LANES] = nxt_re[SUBLANES - 1:SUBLANES, :]
    s5c_ref[0:1, S5_LANES:] = nxt_im[SUBLANES - 1:SUBLANES, :]
    gates.append(gate(2))
    for pos in range(n_pos):
        pr = ptre_ref[prow(pos), :]
        pi = ptim_ref[prow(pos), :]
        st_ref[prow(pos), :S5_LANES] = st_ref[prow(pos), :S5_LANES] + (pr * fr - pi * fi)
        st_ref[prow(pos), S5_LANES:] = st_ref[prow(pos), S5_LANES:] + (pr * fi + pi * fr)
    yc = (_dot(st_ref[:, :S5_LANES].astype(BF16), cbig_ref[:S5_LANES, :])
          + _dot(st_ref[:, S5_LANES:].astype(BF16), cbig_ref[S5_LANES:, :]))
    yc = yc + vrow(_R_S5_D) * u_cm
    yc = jax.nn.gelu(yc)
    yc = yc * jax.nn.sigmoid(_dot(yc.astype(BF16), glu_ref[...]) + vrow(_R_GLU_B))
    for pos in range(n_pos):
        store_time_major(yc_ref, pos, yc[prow(pos), :])
    yc = get(yc_ref)

    gates.append(gate(3))

    ext_d[HALO:HALO + ts, :] = l_x
    xc = ext_d[HALO - 3:HALO - 3 + ts, :] * vrow(_R_CD0)
    xc = xc + ext_d[HALO - 2:HALO - 2 + ts, :] * vrow(_R_CD1)
    xc = xc + ext_d[HALO - 1:HALO - 1 + ts, :] * vrow(_R_CD2)
    xc = xc + ext_d[HALO:HALO + ts, :] * vrow(_R_CD3)
    xc = xc + vrow(_R_CONV_B)
    ext_d[0:HALO, :] = ext_d[ts:ts + HALO, :]
    put(xc_ref, xc)
    xc = to_chunk_major(xc_ref)
    xcb = xc.astype(BF16)
    r_gate = jax.nn.sigmoid(_dot(xcb, wa_ref[...]) + vrow(_R_BA))
    i_gate = jax.nn.sigmoid(_dot(xcb, wx_ref[...]) + vrow(_R_BX))
    z = -vrow(_R_LAM)
    softplus = jnp.maximum(z, 0.0) + jnp.log1p(jnp.exp(-jnp.abs(z)))
    log_a = -LRU_C * r_gate * softplus
    a_t = jnp.exp(log_a)
    mult = jnp.sqrt(1.0 - a_t * a_t)
    b_t = mult * (i_gate * xc)
    hh = b_t[prow(0), :]
    aa = a_t[prow(0), :]
    h_loc = [hh]
    a_cum = [aa]
    for pos in range(1, n_pos):
        a_pos = a_t[prow(pos), :]
        hh = a_pos * hh + b_t[prow(pos), :]
        aa = a_pos * aa
        h_loc.append(hh)
        a_cum.append(aa)
    row8w = lax.broadcasted_iota(jnp.int32, (SUBLANES, MIX_W), 0)
    f = jnp.where(row8w == 0, lruc_ref[0:1, :], pltpu.roll(hh, 1, axis=0))
    m = pltpu.roll(aa, 1, axis=0)
    for shift in (1, 2, 4):
        keep = row8w >= shift
        f_s = pltpu.roll(f, shift, axis=0)
        m_s = pltpu.roll(m, shift, axis=0)
        f = jnp.where(keep, m * f_s + f, f)
        m = jnp.where(keep, m * m_s, m)
    lruc_ref[0:1, :] = (aa * f + hh)[SUBLANES - 1:SUBLANES, :]
    for pos in range(n_pos):
        store_time_major(lb_ref, pos, h_loc[pos] + a_cum[pos] * f)
    yd = get(lb_ref) * jax.nn.gelu(l_g)

    merged = None
    for k, yk in enumerate((ya, yb, yc, yd)):
        term = gates[k] * _dot(yk.astype(BF16), bw_ref[k])
        merged = term if merged is None else merged + term
    o_ref[...] = h + _dot(merged.astype(BF16), wout_ref[...])


def _const_spec(shape):
    nd = len(shape)
    return pl.BlockSpec(shape, lambda b, s: (0,) * nd, pipeline_mode=pl.Buffered(1))


def _mixer(h, p, batch, seq):
    ts = SEQ_TILE
    ns = seq // ts
    t = batch * seq
    weights = (p["g1"], p["w_in"], p["vec"], p["s5v"], p["wpool"], p["bbig"], p["cbig"], p["glu_w"],
               p["wa"], p["wx"], p["wg"], p["bg"], p["bw"], p["w_out"])
    in_specs = [pl.BlockSpec((ts, D_MODEL), lambda b, s: (b * ns + s, 0))]
    in_specs += [_const_spec(w.shape) for w in weights]
    return pl.pallas_call(
        _mixer_kernel,
        grid=(batch, ns),
        in_specs=in_specs,
        out_specs=pl.BlockSpec((ts, D_MODEL), lambda b, s: (b * ns + s, 0)),
        out_shape=jax.ShapeDtypeStruct((t, D_MODEL), F32),
        scratch_shapes=[
            pltpu.VMEM((HALO + ts, MIX_W), F32),
            pltpu.VMEM((HALO + ts, MIX_W), F32),
            pltpu.VMEM((HALO + ts, MIX_W), F32),
            pltpu.VMEM((ts, 2 * S5_LANES), F32),
            pltpu.VMEM((2, ts, LANES), F32),
            pltpu.VMEM((2, ts, LANES), F32),
            pltpu.VMEM((2, ts, LANES), F32),
            pltpu.VMEM((2, ts, LANES), F32),
            pltpu.VMEM((SUBLANES, 2 * S5_LANES), F32),
            pltpu.VMEM((SUBLANES, MIX_W), F32),
            pltpu.VMEM((SUBLANES, S5_LANES), F32),
            pltpu.VMEM((SUBLANES, S5_LANES), F32),
            pltpu.VMEM((SUBLANES, S5_LANES), F32),
            pltpu.VMEM((SUBLANES, S5_LANES), F32),
            pltpu.VMEM((3 * SUBLANES, S5_LANES), F32),
            pltpu.VMEM((3 * SUBLANES, S5_LANES), F32),
            pltpu.VMEM((ts, S5_LANES), F32),
            pltpu.VMEM((ts, S5_LANES), F32),
            pltpu.VMEM((SUBLANES, S5_LANES), F32),
        ],
        compiler_params=pltpu.CompilerParams(
            dimension_semantics=("arbitrary", "arbitrary"),
            vmem_limit_bytes=VMEM_LIMIT),
        name="mixer",
    )(h, *weights)


def _router_kernel(h_ref, g2_ref, wr_ref, br_ref, xn_ref, ri_ref, wcol_ref, cnt_ref, carry_ref):
    tm = h_ref.shape[0]
    i = pl.program_id(0)

    @pl.when(i == 0)
    def _():
        carry_ref[...] = jnp.zeros(carry_ref.shape, F32)

    xn = _rms(h_ref[...], g2_ref[...])
    xn_ref[...] = _pack_rows(xn)
    logits = lax.dot_general(wr_ref[...], xn.astype(BF16), (((1,), (1,)), ((), ())),
                             preferred_element_type=F32) + br_ref[...]
    row8 = lax.broadcasted_iota(jnp.int32, (SUBLANES, tm), 0)
    neg_inf = jnp.float32(-jnp.inf)
    gl = jnp.where(row8 < N_GROUPS, logits[0:SUBLANES, :], neg_inf)
    gmax = jnp.max(gl, axis=0, keepdims=True)
    ge = jnp.exp(gl - gmax)
    gp = ge / jnp.sum(ge, axis=0, keepdims=True)
    g_val = jnp.max(gp, axis=0, keepdims=True)
    g_idx = jnp.min(jnp.where(gp == g_val, row8, SUBLANES), axis=0, keepdims=True)
    sel = logits[4 * SUBLANES:5 * SUBLANES, :]
    for g in (2, 1, 0):
        sel = jnp.where(g_idx == g, logits[(g + 1) * SUBLANES:(g + 2) * SUBLANES, :], sel)
    v1 = jnp.max(sel, axis=0, keepdims=True)
    i1 = jnp.min(jnp.where(sel == v1, row8, SUBLANES), axis=0, keepdims=True)
    sel2 = jnp.where(row8 == i1, neg_inf, sel)
    v2 = jnp.max(sel2, axis=0, keepdims=True)
    i2 = jnp.min(jnp.where(sel2 == v2, row8, SUBLANES), axis=0, keepdims=True)
    e2 = jnp.exp(v2 - v1)
    denom = 1.0 + e2
    w1 = (1.0 / denom) * g_val
    w2 = (e2 / denom) * g_val
    eid0 = g_idx * EXP_PER_GROUP + i1
    eid1 = g_idx * EXP_PER_GROUP + i2
    e32 = lax.broadcasted_iota(jnp.int32, (N_EXPERTS, tm), 0)
    oh0 = (e32 == eid0).astype(F32)
    oh1 = (e32 == eid1).astype(F32)
    oh = oh0 + oh1
    before = (lax.broadcasted_iota(jnp.int32, (tm, tm), 0)
              < lax.broadcasted_iota(jnp.int32, (tm, tm), 1)).astype(BF16)
    base = _dot(oh.astype(BF16), before) + carry_ref[:, 0:1]
    rank0 = jnp.sum(oh0 * base, axis=0, keepdims=True)
    rank1 = jnp.sum(oh1 * base, axis=0, keepdims=True)
    new_carry = carry_ref[...] + jnp.sum(oh, axis=1, keepdims=True)
    carry_ref[...] = new_carry
    cnt_ref[...] = new_carry.astype(jnp.int32)
    ri_ref[...] = jnp.zeros(ri_ref.shape, jnp.int32)
    ri_ref[0:1, :] = eid0
    ri_ref[1:2, :] = eid1
    ri_ref[2:3, :] = rank0.astype(jnp.int32)
    ri_ref[3:4, :] = rank1.astype(jnp.int32)
    rows = lax.broadcasted_iota(jnp.int32, (ROUTER_ROWS, tm), 0)
    wrows = jnp.where(rows == 0, w1, jnp.where(rows == 1, w2, 0.0))
    wcol_ref[...] = wrows.T


def _router(h, g2, wr_t, br_col):
    t = h.shape[0]
    tm = ROUTE_TILE
    return pl.pallas_call(
        _router_kernel,
        grid=(t // tm,),
        in_specs=[
            pl.BlockSpec((tm, D_MODEL), lambda i: (i, 0)),
            pl.BlockSpec((1, D_MODEL), lambda i: (0, 0)),
            pl.BlockSpec((ROUTER_ROWS, D_MODEL), lambda i: (0, 0)),
            pl.BlockSpec((ROUTER_ROWS, 1), lambda i: (0, 0)),
        ],
        out_specs=[
            pl.BlockSpec((tm, ROW_WORDS), lambda i: (i, 0)),
            pl.BlockSpec((SUBLANES, tm), lambda i: (0, i)),
            pl.BlockSpec((tm, ROUTER_ROWS), lambda i: (i, 0)),
            pl.BlockSpec((N_EXPERTS, 128), lambda i: (0, 0)),
        ],
        out_shape=[
            jax.ShapeDtypeStruct((t, ROW_WORDS), jnp.uint32),
            jax.ShapeDtypeStruct((SUBLANES, t), jnp.int32),
            jax.ShapeDtypeStruct((t, ROUTER_ROWS), F32),
            jax.ShapeDtypeStruct((N_EXPERTS, 128), jnp.int32),
        ],
        scratch_shapes=[pltpu.VMEM((N_EXPERTS, 128), F32)],
        compiler_params=pltpu.CompilerParams(
            dimension_semantics=("arbitrary",), vmem_limit_bytes=VMEM_LIMIT),
        name="router",
    )(h, g2, wr_t, br_col)


def _dest_kernel(ri_ref, ps_ref, o_ref):
    tm = ri_ref.shape[1]
    e32 = lax.broadcasted_iota(jnp.int32, (N_EXPERTS, tm), 0)
    o_ref[...] = jnp.zeros(o_ref.shape, jnp.int32)
    for k in range(2):
        start = jnp.sum(jnp.where(e32 == ri_ref[k:k + 1, :], ps_ref[...], 0.0), axis=0, keepdims=True)
        o_ref[k:k + 1, :] = start.astype(jnp.int32) + ri_ref[2 + k:3 + k, :]


def _dest(route_i, pad_start_col):
    t = route_i.shape[1]
    tm = DEST_TILE
    return pl.pallas_call(
        _dest_kernel,
        grid=(t // tm,),
        in_specs=[
            pl.BlockSpec((SUBLANES, tm), lambda i: (0, i)),
            pl.BlockSpec((N_EXPERTS, 1), lambda i: (0, 0)),
        ],
        out_specs=pl.BlockSpec((SUBLANES, tm), lambda i: (0, i)),
        out_shape=jax.ShapeDtypeStruct((SUBLANES, t), jnp.int32),
        compiler_params=pltpu.CompilerParams(dimension_semantics=("arbitrary",)),
        name="dest",
    )(route_i, pad_start_col)


def _dispatch_kernel(dest_ref, fs_ref, fl_ref, meta_ref, x_ref, xs_ref, zbuf, sem, zsem):
    tm = x_ref.shape[0]
    bm = zbuf.shape[0]
    n_blocks = xs_ref.shape[0] // bm
    i = pl.program_id(0)
    n_tok = pl.num_programs(0) * tm
    base = i * tm

    def pad_copy(off, p):
        return pltpu.make_async_copy(zbuf.at[pl.ds(0, p), :], xs_ref.at[pl.ds(off, p), :], zsem)

    def for_each_fill(act):
        def segment(e, _):
            off = fs_ref[e]
            n = fl_ref[e]
            head = n & (SUBLANES - 1)
            for j in range(SUBLANES - 1):
                @pl.when(j < head)
                def _(off=off, j=j):
                    act(pad_copy(off + j, 1))

            off = pl.multiple_of(off + head, SUBLANES)
            for p in PAD_PIECES:
                if p < SUBLANES:
                    continue
                piece = n & p

                @pl.when(piece != 0)
                def _(off=off, p=p):
                    act(pad_copy(off, p))

                off = pl.multiple_of(off + piece, SUBLANES)
            return 0

        lax.fori_loop(0, N_EXPERTS, segment, 0)

        def tail(b, _):
            act(pad_copy(pl.multiple_of(b * bm, bm), bm))
            return 0

        lax.fori_loop(meta_ref[0], n_blocks, tail, 0)

    @pl.when(i == 0)
    def _():
        zbuf[...] = jnp.zeros(zbuf.shape, jnp.uint32)
        for_each_fill(lambda c: c.start())
        for_each_fill(lambda c: c.wait())

    def row_copy(r, d):
        return pltpu.make_async_copy(x_ref.at[pl.ds(r, 1), :], xs_ref.at[pl.ds(d, 1), :], sem)

    for r in range(tm):
        row_copy(r, dest_ref[base + r]).start(priority=0)
        row_copy(r, dest_ref[n_tok + base + r]).start(priority=1)

    for _ in range(2):
        pltpu.make_async_copy(x_ref, xs_ref.at[pl.ds(0, tm), :], sem).wait()


def _dispatch(dest_flat, fill_start, fill_len, meta, xn, n_rows):
    t = xn.shape[0]
    tm = DISPATCH_TILE
    return pl.pallas_call(
        _dispatch_kernel,
        grid_spec=pltpu.PrefetchScalarGridSpec(
            num_scalar_prefetch=4,
            grid=(t // tm,),
            in_specs=[pl.BlockSpec((tm, ROW_WORDS), lambda i, *_: (i, 0))],
            out_specs=pl.BlockSpec(memory_space=pl.ANY),
            scratch_shapes=[
                pltpu.VMEM((EXPERT_BLOCK, ROW_WORDS), jnp.uint32),
                pltpu.SemaphoreType.DMA(()),
                pltpu.SemaphoreType.DMA(()),
            ],
        ),
        out_shape=jax.ShapeDtypeStruct((n_rows, ROW_WORDS), jnp.uint32),
        compiler_params=pltpu.CompilerParams(dimension_semantics=("arbitrary",)),
        name="dispatch",
    )(dest_flat, fill_start, fill_len, meta, xn)


def _expert_kernel(ss_ref, nc_ref, meta_ref, xs_ref, w1_ref, w3_ref, w2_ref, y_ref,
                   xbuf, ybuf, w13b, w2b, insem, outsem):
    e = pl.program_id(0)
    ch = xbuf.shape[1]
    n_blocks = y_ref.shape[0] // ch
    start = ss_ref[e]
    nchunk = nc_ref[e]

    n_in = xbuf.shape[0]
    n_exp = pl.num_programs(0)

    def rows(seg_start, c):
        return pl.ds(pl.multiple_of(seg_start + c * ch, ch), ch)

    def in_copy(seg_start, c, slot):
        return pltpu.make_async_copy(xs_ref.at[rows(seg_start, c), :], xbuf.at[slot], insem.at[slot])

    def out_copy(c, slot):
        return pltpu.make_async_copy(ybuf.at[slot], y_ref.at[rows(start, c), :], outsem.at[slot])

    def prime(seg_start, seg_chunks):
        for k in range(n_in):
            @pl.when(k < seg_chunks)
            def _(k=k):
                in_copy(seg_start, k, k).start(priority=CHUNK_DMA_PRIORITY)

    @pl.when(e == 0)
    def _():
        prime(start, nchunk)

    w13b[:, :D_EXPERT] = w1_ref[0, 0].astype(BF16)
    w13b[:, D_EXPERT:] = w3_ref[0, 0].astype(BF16)
    w2b[...] = w2_ref[0, 0].astype(BF16)

    def body(c, _):
        slot = c % 2
        islot = c % n_in
        in_copy(start, c, islot).wait()

        @pl.when(c >= 2)
        def _():
            out_copy(c - 2, slot).wait()

        lo, hi = _unpack_rows(xbuf[islot])
        xb = jnp.concatenate([lo.astype(BF16), hi.astype(BF16)], axis=1)
        gu = _dot(xb, w13b[...])
        hid = jax.nn.silu(gu[:, :D_EXPERT]) * gu[:, D_EXPERT:]
        ybuf[slot] = _pack_rows(_dot(hid.astype(BF16), w2b[...]))
        out_copy(c, slot).start(priority=CHUNK_DMA_PRIORITY)

        @pl.when(c + n_in < nchunk)
        def _():
            in_copy(start, c + n_in, islot).start(priority=CHUNK_DMA_PRIORITY)

        return 0

    lax.fori_loop(0, nchunk, body, 0)

    @pl.when(e + 1 < n_exp)
    def _():
        nxt = jnp.minimum(e + 1, n_exp - 1)
        prime(ss_ref[nxt], nc_ref[nxt])

    @pl.when(nchunk >= 2)
    def _():
        out_copy(nchunk - 2, nchunk % 2).wait()

    @pl.when(nchunk >= 1)
    def _():
        out_copy(nchunk - 1, (nchunk - 1) % 2).wait()

    @pl.when(e == pl.num_programs(0) - 1)
    def _():
        ybuf[0] = jnp.zeros(ybuf.shape[1:], jnp.uint32)

        def tail_copy(b):
            return pltpu.make_async_copy(
                ybuf.at[0], y_ref.at[pl.ds(pl.multiple_of(b * ch, ch), ch), :], outsem.at[0])

        def tail_start(b, _):
            tail_copy(b).start()
            return 0

        def tail_wait(b, _):
            tail_copy(b).wait()
            return 0

        lax.fori_loop(meta_ref[0], n_blocks, tail_start, 0)
        lax.fori_loop(meta_ref[0], n_blocks, tail_wait, 0)


def _experts(layer, seg_start, seg_chunks, meta, xs, w1, w3, w2):
    n_rows = xs.shape[0]
    ch = EXPERT_BLOCK

    def w_map(e, *_):
        return (layer, e, 0, 0)

    return pl.pallas_call(
        _expert_kernel,
        grid_spec=pltpu.PrefetchScalarGridSpec(
            num_scalar_prefetch=3,
            grid=(N_EXPERTS,),
            in_specs=[
                pl.BlockSpec(memory_space=pl.ANY),
                pl.BlockSpec((1, 1, D_MODEL, D_EXPERT), w_map),
                pl.BlockSpec((1, 1, D_MODEL, D_EXPERT), w_map),
                pl.BlockSpec((1, 1, D_EXPERT, D_MODEL), w_map),
            ],
            out_specs=pl.BlockSpec(memory_space=pl.ANY),
            scratch_shapes=[
                pltpu.VMEM((EXPERT_IN_BUFFERS, ch, ROW_WORDS), jnp.uint32),
                pltpu.VMEM((2, ch, ROW_WORDS), jnp.uint32),
                pltpu.VMEM((D_MODEL, 2 * D_EXPERT), BF16),
                pltpu.VMEM((D_EXPERT, D_MODEL), BF16),
                pltpu.SemaphoreType.DMA((EXPERT_IN_BUFFERS,)),
                pltpu.SemaphoreType.DMA((2,)),
            ],
        ),
        out_shape=jax.ShapeDtypeStruct((n_rows, ROW_WORDS), jnp.uint32),
        compiler_params=pltpu.CompilerParams(
            dimension_semantics=("arbitrary",), vmem_limit_bytes=VMEM_LIMIT),
        name="experts",
    )(seg_start, seg_chunks, meta, xs, w1, w3, w2)


def _combine_kernel(dest_ref, h_ref, wcol_ref, g_ref, y_ref, o_ref, buf, sem, *, final_norm):
    tc = h_ref.shape[0]
    i = pl.program_id(0)
    n = pl.num_programs(0)

    def row_copy(p, slot, k, r):
        return pltpu.make_async_copy(y_ref.at[pl.ds(p, 1), :],
                                     buf.at[2 * slot + k, pl.ds(r, 1), :], sem.at[slot])

    n_tok = n * tc

    def issue(tile, slot):
        tok0 = tile * tc
        for r in range(tc):
            row_copy(dest_ref[tok0 + r], slot, 0, r).start(priority=0)
            row_copy(dest_ref[n_tok + tok0 + r], slot, 1, r).start(priority=1)

    @pl.when(i == 0)
    def _():
        issue(0, 0)

    @pl.when(i + 1 < n)
    def _():
        issue(i + 1, (i + 1) % 2)

    slot = i % 2

    for k in range(2):
        pltpu.make_async_copy(y_ref.at[pl.ds(0, tc), :], buf.at[2 * slot + k], sem.at[slot]).wait()
    w = wcol_ref[...]
    lo0, hi0 = _unpack_rows(buf[2 * slot])
    lo1, hi1 = _unpack_rows(buf[2 * slot + 1])
    moe = jnp.concatenate([w[:, 0:1] * lo0 + w[:, 1:2] * lo1,
                           w[:, 0:1] * hi0 + w[:, 1:2] * hi1], axis=1)
    out = h_ref[...] + moe
    if final_norm:
        out = _rms(out, g_ref[...])
    o_ref[...] = out


def _combine(dest_flat, h, wcol, g, y, final_norm):
    t = h.shape[0]
    tc = COMBINE_TILE
    return pl.pallas_call(
        functools.partial(_combine_kernel, final_norm=final_norm),
        grid_spec=pltpu.PrefetchScalarGridSpec(
            num_scalar_prefetch=1,
            grid=(t // tc,),
            in_specs=[
                pl.BlockSpec((tc, D_MODEL), lambda i, d: (i, 0)),
                pl.BlockSpec((tc, ROUTER_ROWS), lambda i, d: (i, 0)),
                pl.BlockSpec((1, D_MODEL), lambda i, d: (0, 0)),
                pl.BlockSpec(memory_space=pl.ANY),
            ],
            out_specs=pl.BlockSpec((tc, D_MODEL), lambda i, d: (i, 0)),
            scratch_shapes=[
                pltpu.VMEM((4, tc, ROW_WORDS), jnp.uint32),
                pltpu.SemaphoreType.DMA((2,)),
            ],
        ),
        out_shape=jax.ShapeDtypeStruct((t, D_MODEL), F32),
        compiler_params=pltpu.CompilerParams(
            dimension_semantics=("arbitrary",), vmem_limit_bytes=VMEM_LIMIT),
        name="combine",
    )(dest_flat, h, wcol, g, y)


def _block_diag(w):
    g, i, o = w.shape
    eye = jnp.eye(g, dtype=w.dtype)
    return jnp.einsum("gio,gk->giko", w, eye).reshape(g * i, g * o)


def _layer_params(l, a):
    row = lambda v: v.reshape(1, -1)
    vec_rows = [a["pool_b"][l].reshape(-1), a["pool_scale"][l], a["s5_d"][l], a["s5_glu_b"][l],
                a["lru_conv_b"][l], a["lru_ba"][l].reshape(-1), a["lru_bx"][l].reshape(-1),
                a["lru_lambda"][l],
                a["conv_a_w"][l][0], a["conv_a_w"][l][1], a["conv_a_w"][l][2],
                a["lru_conv_w"][l][0], a["lru_conv_w"][l][1], a["lru_conv_w"][l][2],
                a["lru_conv_w"][l][3], jnp.zeros((MIX_W,), F32)]
    s5v = jnp.stack([a["s5_lambda_re"][l].reshape(-1), a["s5_lambda_im"][l].reshape(-1),
                     jnp.repeat(a["s5_log_step"][l], S5_STATE)]
                    + [jnp.zeros((S5_LANES,), F32)] * 5)
    b_re = _block_diag(jnp.swapaxes(a["s5_b_re"][l], 1, 2))
    b_im = _block_diag(jnp.swapaxes(a["s5_b_im"][l], 1, 2))
    c_re = _block_diag(jnp.swapaxes(a["s5_c_re"][l], 1, 2))
    c_im = _block_diag(jnp.swapaxes(a["s5_c_im"][l], 1, 2))
    wr_t = jnp.zeros((ROUTER_ROWS, D_MODEL), F32)
    wr_t = wr_t.at[0:N_GROUPS].set(a["router_group_w"][l].T)
    wr_t = wr_t.at[SUBLANES:SUBLANES + N_EXPERTS].set(a["router_expert_w"][l].T)
    br = jnp.zeros((ROUTER_ROWS,), F32)
    br = br.at[0:N_GROUPS].set(a["router_group_b"][l])
    br = br.at[SUBLANES:SUBLANES + N_EXPERTS].set(a["router_expert_b"][l])
    return {
        "g1": row(a["norm1_g"][l]),
        "w_in": a["w_in"][l].astype(BF16),
        "vec": jnp.stack(vec_rows),
        "s5v": s5v,
        "wpool": _block_diag(a["pool_w"][l]).astype(BF16),
        "bbig": jnp.concatenate([b_re, b_im], axis=1).astype(BF16),
        "cbig": jnp.concatenate([c_re, -c_im], axis=0).astype(BF16),
        "glu_w": a["s5_glu_w"][l].astype(BF16),
        "wa": _block_diag(a["lru_wa"][l]).astype(BF16),
        "wx": _block_diag(a["lru_wx"][l]).astype(BF16),
        "wg": a["merge_gate_w"][l].astype(BF16),
        "bg": row(a["merge_gate_b"][l]),
        "bw": a["branch_w"][l].astype(BF16),
        "w_out": a["w_out"][l].astype(BF16),
        "g2": row(a["norm2_g"][l]),
        "wr_t": wr_t.astype(BF16),
        "br": br.reshape(ROUTER_ROWS, 1),
    }


def _moe(layer, h, p, w1, w3, w2, g_final, final_norm):
    t = h.shape[0]
    bm = EXPERT_BLOCK
    xn, route_i, wcol, counts = _router(h, p["g2"], p["wr_t"], p["br"])
    cnt = counts[:, 0]
    padded = ((cnt + bm - 1) // bm) * bm
    pad_end = jnp.cumsum(padded)
    pad_start = pad_end - padded
    n_rows = (-(-(2 * t) // bm)) * bm + N_EXPERTS * bm
    n_blocks = n_rows // bm
    meta = (pad_end[-1:] // bm).astype(jnp.int32)
    dest = _dest(route_i, pad_start.astype(F32).reshape(N_EXPERTS, 1))
    dest_flat = dest[0:2].reshape(-1)
    xs = _dispatch(dest_flat, (pad_start + cnt).astype(jnp.int32), (padded - cnt).astype(jnp.int32),
                   meta, xn, n_rows)
    y = _experts(layer, pad_start.astype(jnp.int32), (padded // bm).astype(jnp.int32), meta,
                 xs, w1, w3, w2)
    return _combine(dest_flat, h, wcol, g_final, y, final_norm)


def kernel(x, norm1_g, w_in, conv_a_w, pool_w, pool_b, pool_scale, s5_lambda_re, s5_lambda_im, s5_log_step, s5_b_re, s5_b_im, s5_c_re, s5_c_im, s5_d, s5_glu_w, s5_glu_b, lru_conv_w, lru_conv_b, lru_wa, lru_ba, lru_wx, lru_bx, lru_lambda, merge_gate_w, merge_gate_b, branch_w, w_out, norm2_g, router_group_w, router_group_b, router_expert_w, router_expert_b, expert_w1, expert_w3, expert_w2, final_norm_g):
    a = dict(norm1_g=norm1_g, w_in=w_in, conv_a_w=conv_a_w, pool_w=pool_w, pool_b=pool_b,
             pool_scale=pool_scale, s5_lambda_re=s5_lambda_re, s5_lambda_im=s5_lambda_im,
             s5_log_step=s5_log_step, s5_b_re=s5_b_re, s5_b_im=s5_b_im, s5_c_re=s5_c_re,
             s5_c_im=s5_c_im, s5_d=s5_d, s5_glu_w=s5_glu_w, s5_glu_b=s5_glu_b,
             lru_conv_w=lru_conv_w, lru_conv_b=lru_conv_b, lru_wa=lru_wa, lru_ba=lru_ba,
             lru_wx=lru_wx, lru_bx=lru_bx, lru_lambda=lru_lambda, merge_gate_w=merge_gate_w,
             merge_gate_b=merge_gate_b, branch_w=branch_w, w_out=w_out, norm2_g=norm2_g,
             router_group_w=router_group_w, router_group_b=router_group_b,
             router_expert_w=router_expert_w, router_expert_b=router_expert_b)
    batch, seq, d = x.shape
    depth = norm1_g.shape[0]
    h = x.reshape(batch * seq, d)
    g_final = final_norm_g.reshape(1, d)
    for l in range(depth):
        p = _layer_params(l, a)
        h = _mixer(h, p, batch, seq)
        h = _moe(l, h, p, expert_w1, expert_w3, expert_w2, g_final, l == depth - 1)
    return h.reshape(batch, seq, d)
```
